```python
import math
import jax, jax.numpy as jnp
from jax import lax
import numpy as np

D_MODEL = 1024
BATCH = 16
SEQ = 4096
DEPTH = 4

F32 = jnp.float32
N_BRANCHES = 4
BRANCH_WIDTH = 512
NORM_EPS = 1e-6
CHUNK = 64
GLA_HEADS = 4
GLA_DK = 64
GLA_DV = BRANCH_WIDTH // GLA_HEADS
GLA_KEY = GLA_HEADS * GLA_DK
GLA_RANK = 16
GLA_GATE_TEMP = 16.0
RET_HEADS = 4
RET_DK = 64
RET_DV = BRANCH_WIDTH // RET_HEADS
RET_KEY = RET_HEADS * RET_DK
ROPE_BASE = 10000.0
LRU_BLOCKS = 8
LRU_BLOCK_DIM = BRANCH_WIDTH // LRU_BLOCKS
CONV_WIDTH = 4
LRU_C = 8.0
S5_GROUP = 16
S5_GROUPS = BRANCH_WIDTH // S5_GROUP
S5_STATE = 64
S5_BLOCK = 128
IN_SIZES = (GLA_KEY, GLA_KEY, BRANCH_WIDTH, GLA_RANK, BRANCH_WIDTH,
            RET_KEY, RET_KEY, BRANCH_WIDTH, BRANCH_WIDTH,
            BRANCH_WIDTH, BRANCH_WIDTH,
            BRANCH_WIDTH, BRANCH_WIDTH)
IN_COLS = sum(IN_SIZES)

kernel_name = 'hybrid_gla_retnet_rglru_s5_gated_merge'


def rmsnorm(x, gain):
    xf = x.astype(F32)
    y = xf * lax.rsqrt(jnp.mean(xf * xf, axis=-1, keepdims=True) + NORM_EPS)
    return (y * gain.astype(F32)).astype(x.dtype)


def split_columns(proj):
    offsets, acc = [], 0
    for s in IN_SIZES[:-1]:
        acc += s
        offsets.append(acc)
    return jnp.split(proj, offsets, axis=-1)


def to_chunks(t, heads):
    b, s, w = t.shape
    return t.reshape(b, s // CHUNK, CHUNK, heads, w // heads).transpose(0, 3, 1, 2, 4)


def from_chunks(t):
    b, h, n, c, d = t.shape
    return t.transpose(0, 2, 3, 1, 4).reshape(b, n * c, h * d)


def scan_chunk_states(kv, decay):
    def step(state, inp):
        kv_n, dec_n = inp
        return state * dec_n + kv_n, state
    init = jnp.zeros(kv.shape[:2] + kv.shape[3:], F32)
    _, prev = lax.scan(step, init, (jnp.moveaxis(kv, 2, 0), jnp.moveaxis(decay, 2, 0)))
    return jnp.moveaxis(prev, 0, 2)


def gla_mix(q, k, v, lr, w_lr, b_lr, norm_gain):
    z = jnp.einsum('bsr,rk->bsk', lr.astype(F32), w_lr.astype(F32)) + b_lr.astype(F32)
    log_a = jax.nn.log_sigmoid(z) / GLA_GATE_TEMP
    qc = to_chunks(q.astype(F32), GLA_HEADS) * GLA_DK ** -0.5
    kc = to_chunks(k.astype(F32), GLA_HEADS)
    vc = to_chunks(v.astype(F32), GLA_HEADS)
    cum = jnp.cumsum(to_chunks(log_a, GLA_HEADS), axis=3)
    last = cum[..., -1:, :]
    q_dec = qc * jnp.exp(cum)
    k_dec = kc * jnp.exp(-cum)
    k_tail = kc * jnp.exp(last - cum)
    causal = jnp.tril(jnp.ones((CHUNK, CHUNK), dtype=bool))
    scores = jnp.where(causal, jnp.einsum('bhnid,bhnjd->bhnij', q_dec, k_dec), 0.0)
    o = jnp.einsum('bhnij,bhnjv->bhniv', scores, vc)
    kv = jnp.einsum('bhncd,bhncv->bhndv', k_tail, vc)
    s_prev = scan_chunk_states(kv, jnp.exp(last[..., 0, :])[..., None])
    o = o + jnp.einsum('bhncd,bhndv->bhncv', q_dec, s_prev)
    o = o * lax.rsqrt(jnp.mean(o * o, axis=-1, keepdims=True) + NORM_EPS)
    o = o * norm_gain.astype(F32).reshape(GLA_HEADS, 1, 1, GLA_DV)
    return from_chunks(o)


def rotary(t):
    s, d = t.shape[1], t.shape[-1]
    half = d // 2
    inv = ROPE_BASE ** (-jnp.arange(half, dtype=F32) / half)
    ang = jnp.arange(s, dtype=F32)[:, None] * inv[None, :]
    cos = jnp.cos(ang)[None, :, None, :]
    sin = jnp.sin(ang)[None, :, None, :]
    t1, t2 = t[..., :half], t[..., half:]
    return jnp.concatenate([t1 * cos - t2 * sin, t1 * sin + t2 * cos], axis=-1)


def retention_mix(q, k, v, gn_gain, gn_bias):
    b, s, _ = q.shape
    n = s // CHUNK
    qr = rotary(q.astype(F32).reshape(b, s, RET_HEADS, RET_DK)).reshape(b, s, RET_KEY)
    kr = rotary(k.astype(F32).reshape(b, s, RET_HEADS, RET_DK)).reshape(b, s, RET_KEY) * RET_DK ** -0.5
    qc = to_chunks(qr, RET_HEADS)
    kc = to_chunks(kr, RET_HEADS)
    vc = to_chunks(v.astype(F32), RET_HEADS)
    log_gamma = jnp.log1p(-jnp.exp2(-5.0 - jnp.arange(RET_HEADS, dtype=F32)))
    pos = jnp.arange(CHUNK, dtype=F32)
    rel = pos[:, None] - pos[None, :]
    intra = jnp.where(rel >= 0, jnp.exp(jnp.maximum(rel, 0.0)[None] * log_gamma[:, None, None]), 0.0)
    scores = jnp.einsum('bhnid,bhnjd->bhnij', qc, kc) * intra[:, None]
    o = jnp.einsum('bhnij,bhnjv->bhniv', scores, vc)
    k_w = jnp.exp((CHUNK - 1.0 - pos)[None, :] * log_gamma[:, None])
    kv = jnp.einsum('bhncd,hc,bhncv->bhndv', kc, k_w, vc)
    chunk_decay = jnp.broadcast_to(jnp.exp(CHUNK * log_gamma)[None, :, None, None, None], (1, RET_HEADS, n, 1, 1))
    s_prev = scan_chunk_states(kv, chunk_decay)
    q_w = jnp.exp((pos + 1.0)[None, :] * log_gamma[:, None])
    o = o + jnp.einsum('bhncd,bhndv,hc->bhncv', qc, s_prev, q_w)
    mean = jnp.mean(o, axis=-1, keepdims=True)
    var = jnp.mean(jnp.square(o - mean), axis=-1, keepdims=True)
    o = (o - mean) * lax.rsqrt(var + NORM_EPS)
    o = o * gn_gain.astype(F32).reshape(RET_HEADS, 1, 1, RET_DV) + gn_bias.astype(F32).reshape(RET_HEADS, 1, 1, RET_DV)
    return from_chunks(o)


def linear_combine(left, right):
    a1, b1 = left
    a2, b2 = right
    return a1 * a2, a2 * b1 + b2


def rglru_mix(x, conv_w, conv_b, w_a, b_a, w_x, b_x, lam):
    b, s, w = x.shape
    xf = x.astype(F32)
    xc = lax.conv_general_dilated(xf, conv_w.astype(F32)[:, None, :], window_strides=(1,),
                                  padding=[(CONV_WIDTH - 1, 0)], dimension_numbers=('NWC', 'WIO', 'NWC'),
                                  feature_group_count=w) + conv_b.astype(F32)
    xb = xc.reshape(b, s, LRU_BLOCKS, LRU_BLOCK_DIM)
    r = jax.nn.sigmoid(jnp.einsum('bsni,nij->bsnj', xb, w_a.astype(F32)).reshape(b, s, w) + b_a.astype(F32))
    i = jax.nn.sigmoid(jnp.einsum('bsni,nij->bsnj', xb, w_x.astype(F32)).reshape(b, s, w) + b_x.astype(F32))
    log_a = -LRU_C * r * jax.nn.softplus(-lam.astype(F32))
    a = jnp.exp(log_a)
    gated = xc * i * jnp.sqrt(-jnp.expm1(2.0 * log_a))
    _, h = lax.associative_scan(linear_combine, (a, gated), axis=1)
    return h


def complex_linear_combine(left, right):
    ar1, ai1, br1, bi1 = left
    ar2, ai2, br2, bi2 = right
    return (ar1 * ar2 - ai1 * ai2, ar1 * ai2 + ai1 * ar2,
            ar2 * br1 - ai2 * bi1 + br2, ar2 * bi1 + ai2 * br1 + bi2)


def s5_scan(u, a_re, a_im, b_re, b_im, c_re, c_im, d, log_dt):
    b, s, w = u.shape
    a_re, a_im = a_re.astype(F32), a_im.astype(F32)
    b_re, b_im = b_re.astype(F32), b_im.astype(F32)
    c_re, c_im = c_re.astype(F32), c_im.astype(F32)
    step = jnp.exp(log_dt.astype(F32))[:, None]
    mag = jnp.exp(step * a_re)
    ab_re = mag * jnp.cos(step * a_im)
    ab_im = mag * jnp.sin(step * a_im)
    den = a_re * a_re + a_im * a_im
    f_re = ((ab_re - 1.0) * a_re + ab_im * a_im) / den
    f_im = (ab_im * a_re - (ab_re - 1.0) * a_im) / den
    bb_re = f_re[..., None] * b_re - f_im[..., None] * b_im
    bb_im = f_re[..., None] * b_im + f_im[..., None] * b_re
    d_g = d.astype(F32).reshape(S5_GROUPS, S5_GROUP)
    a_blk_re = jnp.broadcast_to(ab_re, (b, S5_BLOCK, S5_GROUPS, S5_STATE))
    a_blk_im = jnp.broadcast_to(ab_im, (b, S5_BLOCK, S5_GROUPS, S5_STATE))
    ub = jnp.moveaxis(u.astype(F32).reshape(b, s // S5_BLOCK, S5_BLOCK, S5_GROUPS, S5_GROUP), 1, 0)

    def block_step(carry, u_blk):
        h_re, h_im = carry
        bu_re = jnp.einsum('blgi,gpi->blgp', u_blk, bb_re)
        bu_im = jnp.einsum('blgi,gpi->blgp', u_blk, bb_im)
        bu_re = bu_re.at[:, 0].add(ab_re * h_re - ab_im * h_im)
        bu_im = bu_im.at[:, 0].add(ab_re * h_im + ab_im * h_re)
        _, _, hs_re, hs_im = lax.associative_scan(complex_linear_combine, (a_blk_re, a_blk_im, bu_re, bu_im), axis=1)
        y = (jnp.einsum('blgp,gip->blgi', hs_re, c_re) - jnp.einsum('blgp,gip->blgi', hs_im, c_im)
             + d_g * u_blk)
        return (hs_re[:, -1], hs_im[:, -1]), y

    init = (jnp.zeros((b, S5_GROUPS, S5_STATE), F32), jnp.zeros((b, S5_GROUPS, S5_STATE), F32))
    _, ys = lax.scan(block_step, init, ub)
    return jnp.moveaxis(ys, 0, 1).reshape(b, s, w)


def setup_inputs(seed: int = 0) -> dict:
    key = jax.random.key(seed)
    ks = jax.random.split(key, 32)

    def nrm(k, shape, scale):
        return scale * jax.random.normal(k, shape, F32)

    W = BRANCH_WIDTH
    a_pow = jax.random.uniform(ks[14], (DEPTH, W), F32, 0.9, 0.999)
    p = a_pow ** (1.0 / LRU_C)
    return {
        'x': nrm(ks[0], (BATCH, SEQ, D_MODEL), 1.0),
        'norm_gain': 1.0 + nrm(ks[1], (DEPTH, D_MODEL), 0.02),
        'w_in': nrm(ks[2], (DEPTH, D_MODEL, IN_COLS), D_MODEL ** -0.5),
        'gla_w_lr': nrm(ks[3], (DEPTH, GLA_RANK, GLA_KEY), GLA_RANK ** -0.5),
        'gla_b_lr': nrm(ks[4], (DEPTH, GLA_KEY), 0.1),
        'gla_norm_gain': 1.0 + nrm(ks[5], (DEPTH, W), 0.02),
        'ret_norm_gain': 1.0 + nrm(ks[6], (DEPTH, W), 0.02),
        'ret_norm_bias': nrm(ks[7], (DEPTH, W), 0.02),
        'lru_conv_w': nrm(ks[8], (DEPTH, CONV_WIDTH, W), CONV_WIDTH ** -0.5),
        'lru_conv_b': nrm(ks[9], (DEPTH, W), 0.02),
        'lru_w_a': nrm(ks[10], (DEPTH, LRU_BLOCKS, LRU_BLOCK_DIM, LRU_BLOCK_DIM), LRU_BLOCK_DIM ** -0.5),
        'lru_b_a': nrm(ks[11], (DEPTH, W), 0.02),
        'lru_w_x': nrm(ks[12], (DEPTH, LRU_BLOCKS, LRU_BLOCK_DIM, LRU_BLOCK_DIM), LRU_BLOCK_DIM ** -0.5),
        'lru_b_x': nrm(ks[13], (DEPTH, W), 0.02),
        'lru_lambda': jnp.log(p) - jnp.log1p(-p),
        's5_a_re': -0.5 + nrm(ks[15], (DEPTH, S5_GROUPS, S5_STATE), 0.01),
        's5_a_im': jnp.pi * jnp.arange(S5_STATE, dtype=F32) + nrm(ks[16], (DEPTH, S5_GROUPS, S5_STATE), 0.01),
        's5_b_re': nrm(ks[17], (DEPTH, S5_GROUPS, S5_STATE, S5_GROUP), (2.0 * S5_GROUP) ** -0.5),
        's5_b_im': nrm(ks[18], (DEPTH, S5_GROUPS, S5_STATE, S5_GROUP), (2.0 * S5_GROUP) ** -0.5),
        's5_c_re': nrm(ks[19], (DEPTH, S5_GROUPS, S5_GROUP, S5_STATE), (2.0 * S5_STATE) ** -0.5),
        's5_c_im': nrm(ks[20], (DEPTH, S5_GROUPS, S5_GROUP, S5_STATE), (2.0 * S5_STATE) ** -0.5),
        's5_d': nrm(ks[21], (DEPTH, W), 1.0),
        's5_log_dt': jax.random.uniform(ks[22], (DEPTH, S5_GROUPS), F32, math.log(0.001), math.log(0.1)),
        's5_glu_w': nrm(ks[23], (DEPTH, W, W), W ** -0.5),
        's5_glu_b': nrm(ks[24], (DEPTH, W), 0.02),
        'w_merge_gate': nrm(ks[25], (DEPTH, N_BRANCHES, D_MODEL, D_MODEL), D_MODEL ** -0.5),
        'w_branch': nrm(ks[26], (DEPTH, N_BRANCHES, W, D_MODEL), W ** -0.5),
        'w_out': nrm(ks[27], (DEPTH, D_MODEL, D_MODEL), D_MODEL ** -0.5),
        'final_norm_gain': 1.0 + nrm(ks[28], (D_MODEL,), 0.02),
    }


def reference(x, norm_gain, w_in, gla_w_lr, gla_b_lr, gla_norm_gain, ret_norm_gain, ret_norm_bias,
              lru_conv_w, lru_conv_b, lru_w_a, lru_b_a, lru_w_x, lru_b_x, lru_lambda,
              s5_a_re, s5_a_im, s5_b_re, s5_b_im, s5_c_re, s5_c_im, s5_d, s5_log_dt, s5_glu_w, s5_glu_b,
              w_merge_gate, w_branch, w_out, final_norm_gain):
    dt = x.dtype
    h = x
    for l in range(DEPTH):
        u = rmsnorm(h, norm_gain[l])
        proj = u @ w_in[l]
        (gq, gk, gv, glr, gg, rq, rk, rv, rg, lx, lg, su, sg) = split_columns(proj)
        o_gla = gla_mix(gq, gk, gv, glr, gla_w_lr[l], gla_b_lr[l], gla_norm_gain[l]).astype(dt) * jax.nn.silu(gg)
        o_ret = retention_mix(rq, rk, rv, ret_norm_gain[l], ret_norm_bias[l]).astype(dt) * jax.nn.silu(rg)
        o_lru = rglru_mix(lx, lru_conv_w[l], lru_conv_b[l], lru_w_a[l], lru_b_a[l], lru_w_x[l], lru_b_x[l],
                          lru_lambda[l]).astype(dt) * jax.nn.silu(lg)
        y5 = jax.nn.gelu(s5_scan(su, s5_a_re[l], s5_a_im[l], s5_b_re[l], s5_b_im[l], s5_c_re[l], s5_c_im[l],
                                 s5_d[l], s5_log_dt[l])).astype(dt)
        y5 = y5 * jax.nn.sigmoid(y5 @ s5_glu_w[l] + s5_glu_b[l])
        o_s5 = y5 * jax.nn.silu(sg)
        merged = jnp.zeros_like(h)
        for bi, o in enumerate((o_gla, o_ret, o_lru, o_s5)):
            gate = jax.nn.sigmoid(u @ w_merge_gate[l, bi])
            merged = merged + gate * (o @ w_branch[l, bi])
        h = h + merged @ w_out[l]
    return rmsnorm(h, final_norm_gain)
```

```python
import functools
import math

import jax
import jax.numpy as jnp
from jax import lax
from jax.experimental import pallas as pl
from jax.experimental.pallas import tpu as pltpu

F32 = jnp.float32
BF16 = jnp.bfloat16

D_MODEL = 1024
N_BRANCHES = 4
WIDTH = 512
NORM_EPS = 1e-6
CHUNK = 64
HEADS = 4
DK = 64
DV = WIDTH // HEADS
KEY = HEADS * DK
GLA_RANK = 16
GLA_GATE_TEMP = 16.0
ROPE_BASE = 10000.0
LRU_BLOCKS = 8
LRU_BLOCK_DIM = WIDTH // LRU_BLOCKS
CONV_WIDTH = 4
LRU_C = 8.0
S5_GROUP = 16
S5_GROUPS = WIDTH // S5_GROUP
S5_STATE = 64

LANES = 128
SUBLANES = 8
VMEM_LIMIT_BYTES = 60 * 1024 * 1024

C_GQ, C_GK, C_GV, C_GG = 0, 256, 512, 1024
C_RQ, C_RK, C_RV, C_RG = 1536, 1792, 2048, 2560
C_LX, C_LG, C_SU, C_SG = 3072, 3584, 4096, 4608
C_LR = 5120
N_COLS = C_LR + LANES
PROJ_BLOCK = 512
N_SLABS = WIDTH // LANES
S5_SLAB_STATE = (LANES // S5_GROUP) * S5_STATE

TILE = 512
SUB = 256


def _sigmoid(x):
    return jax.nn.sigmoid(x)


def _silu(x):
    return x * jax.nn.sigmoid(x)


def _gelu_tanh(x):
    c = math.sqrt(2.0 / math.pi)
    return 0.5 * x * (1.0 + jnp.tanh(c * (x + 0.044715 * (x * x * x))))


def _dot(a, b):
    return jnp.dot(a, b, preferred_element_type=F32)


def _dot_nt(a, b):
    return lax.dot_general(a, b, (((1,), (1,)), ((), ())), preferred_element_type=F32)


def _dot_tn(a, b):
    return lax.dot_general(a, b, (((0,), (0,)), ((), ())), preferred_element_type=F32)


def _rmsnorm_rows(x, gain):
    return x * lax.rsqrt(jnp.mean(x * x, axis=-1, keepdims=True) + NORM_EPS) * gain


def _mixers_kernel(h_ref, ng_ref, win_ref, wlr_ref, blr_ref, gng_ref, rng_ref, rnb_ref,
                   cos_ref, sin_ref, tri_ref, bones_ref, cmask_ref, rmask_ref, rkw_ref, rqw_ref, rdec_ref,
                   convw_ref, convb_ref, wlru_ref, ba_ref, bx_ref, lam_ref,
                   wb_ref, wc_ref, s5tab_ref, s5d_ref, gluw_ref, glub_ref,
                   ocat_ref,
                   ub_ref, proj_ref, gst_ref, rst_ref, lxbuf_ref, xc_ref, la_ref, lg_ref, lcar_ref,
                   xs_ref, scar_ref, y5_ref):
    tile = h_ref.shape[0]
    n_blocks = tile // SUBLANES
    j = pl.program_id(1)

    @pl.when(j == 0)
    def _():
        gst_ref[...] = jnp.zeros_like(gst_ref)
        rst_ref[...] = jnp.zeros_like(rst_ref)
        lxbuf_ref[0:SUBLANES, :] = jnp.zeros((SUBLANES, WIDTH), F32)
        lcar_ref[...] = jnp.zeros_like(lcar_ref)
        scar_ref[...] = jnp.zeros_like(scar_ref)

    @pl.when(j != 0)
    def _():
        lxbuf_ref[0:SUBLANES, :] = lxbuf_ref[tile:tile + SUBLANES, :]

    x = h_ref[...]
    ub_ref[...] = _rmsnorm_rows(x, ng_ref[...]).astype(BF16)

    def proj_body(n, carry):
        c0 = pl.multiple_of(n * PROJ_BLOCK, PROJ_BLOCK)
        proj_ref[:, pl.ds(c0, PROJ_BLOCK)] = _dot(ub_ref[...], win_ref[:, pl.ds(c0, PROJ_BLOCK)])
        return carry

    lax.fori_loop(0, C_LR // PROJ_BLOCK, proj_body, 0)
    proj_ref[:, C_LR:N_COLS] = _dot(ub_ref[...], win_ref[:, C_LR:N_COLS])

    lane = lax.broadcasted_iota(jnp.int32, (1, LANES), 1)
    head_masks = (lane < DK, lane >= DK)
    cmask = cmask_ref[...] != 0.0

    for st in range(tile // SUB):
        r0 = st * SUB
        rows = slice(r0, r0 + SUB)

        lr = proj_ref[rows, C_LR:C_LR + LANES].astype(BF16)
        z = _dot(lr, wlr_ref[...]) + blr_ref[...]
        log_a = jax.nn.log_sigmoid(z) * (1.0 / GLA_GATE_TEMP)
        la_hi = log_a.astype(BF16)
        la_lo = (log_a - la_hi.astype(F32)).astype(BF16)
        la_hl = jnp.concatenate([la_hi, la_lo], axis=1)
        cs = _dot(tri_ref[...], la_hl)
        cum = cs[:, :KEY] + cs[:, KEY:]
        ts = _dot(bones_ref[...], la_hl)
        tot = ts[:, :KEY] + ts[:, KEY:]
        gq = proj_ref[rows, C_GQ:C_GQ + KEY]
        gk = proj_ref[rows, C_GK:C_GK + KEY]
        q_dec = gq * (DK ** -0.5) * jnp.exp(cum)
        k_dec = gk * jnp.exp(-cum)
        k_tail = gk * jnp.exp(tot - cum)
        e_dec = jnp.exp(tot)

        for hd in range(HEADS):
            pl0 = LANES * (hd // 2)
            hm = head_masks[hd % 2]
            qm = jnp.where(hm, q_dec[:, pl0:pl0 + LANES], 0.0).astype(BF16)
            kd = k_dec[:, pl0:pl0 + LANES].astype(BF16)
            kt = k_tail[:, pl0:pl0 + LANES].astype(BF16)
            vh = proj_ref[rows, C_GV + DV * hd:C_GV + DV * (hd + 1)].astype(BF16)
            sc = jnp.where(cmask, _dot_nt(qm, kd), 0.0)
            o = _dot(sc.astype(BF16), vh)
            state = gst_ref[hd]
            outs = []
            for c in range(SUB // CHUNK):
                cr = slice(CHUNK * c, CHUNK * (c + 1))
                outs.append(o[cr] + _dot_nt(qm[cr], state.astype(BF16)))
                kv_t = _dot_tn(vh[cr], kt[cr])
                state = state * e_dec[CHUNK * c:CHUNK * c + 1, pl0:pl0 + LANES] + kv_t
            gst_ref[hd] = state
            oh = jnp.concatenate(outs, axis=0)
            oh = oh * lax.rsqrt(jnp.mean(oh * oh, axis=-1, keepdims=True) + NORM_EPS)
            oh = oh * gng_ref[:, DV * hd:DV * (hd + 1)]
            gate = proj_ref[rows, C_GG + DV * hd:C_GG + DV * (hd + 1)]
            ocat_ref[rows, DV * hd:DV * (hd + 1)] = (oh * _silu(gate)).astype(BF16)

        lane_k = lax.broadcasted_iota(jnp.int32, (1, KEY), 1)
        first_half = (lane_k % DK) < (DK // 2)
        cos_t = jnp.concatenate([cos_ref[rows, :]] * (KEY // LANES), axis=1)
        sin_t = jnp.concatenate([sin_ref[rows, :]] * (KEY // LANES), axis=1)

        def rope(t):
            swapped = jnp.where(first_half, pltpu.roll(t, KEY - DK // 2, 1), pltpu.roll(t, DK // 2, 1))
            return t * cos_t + swapped * sin_t

        qr = rope(proj_ref[rows, C_RQ:C_RQ + KEY])
        kr = rope(proj_ref[rows, C_RK:C_RK + KEY]) * (DK ** -0.5)
        q_w = qr * rqw_ref[...]
        k_w = kr * rkw_ref[...]
        for hd in range(HEADS):
            pl0 = LANES * (hd // 2)
            hm = head_masks[hd % 2]
            qm = jnp.where(hm, qr[:, pl0:pl0 + LANES], 0.0).astype(BF16)
            qwm = jnp.where(hm, q_w[:, pl0:pl0 + LANES], 0.0).astype(BF16)
            kd = kr[:, pl0:pl0 + LANES].astype(BF16)
            kt = k_w[:, pl0:pl0 + LANES].astype(BF16)
            vh = proj_ref[rows, C_RV + DV * hd:C_RV + DV * (hd + 1)].astype(BF16)
            sc = _dot_nt(qm, kd) * rmask_ref[hd]
            o = _dot(sc.astype(BF16), vh)
            state = rst_ref[hd]
            outs = []
            for c in range(SUB // CHUNK):
                cr = slice(CHUNK * c, CHUNK * (c + 1))
                outs.append(o[cr] + _dot_nt(qwm[cr], state.astype(BF16)))
                kv_t = _dot_tn(vh[cr], kt[cr])
                state = state * rdec_ref[:, pl0:pl0 + LANES] + kv_t
            rst_ref[hd] = state
            oh = jnp.concatenate(outs, axis=0)
            mean = jnp.mean(oh, axis=-1, keepdims=True)
            cen = oh - mean
            var = jnp.mean(cen * cen, axis=-1, keepdims=True)
            oh = cen * lax.rsqrt(var + NORM_EPS)
            oh = oh * rng_ref[:, DV * hd:DV * (hd + 1)] + rnb_ref[:, DV * hd:DV * (hd + 1)]
            gate = proj_ref[rows, C_RG + DV * hd:C_RG + DV * (hd + 1)]
            ocat_ref[rows, WIDTH + DV * hd:WIDTH + DV * (hd + 1)] = (oh * _silu(gate)).astype(BF16)

    row_id = lax.broadcasted_iota(jnp.int32, (SUBLANES, 1), 0)
    lxbuf_ref[SUBLANES:SUBLANES + tile, :] = proj_ref[:, C_LX:C_LX + WIDTH]
    xc = convb_ref[...] + convw_ref[0:1, :] * lxbuf_ref[SUBLANES - 3:SUBLANES - 3 + tile, :]
    for tap in range(1, CONV_WIDTH):
        off = SUBLANES - (CONV_WIDTH - 1) + tap
        xc = xc + convw_ref[tap:tap + 1, :] * lxbuf_ref[off:off + tile, :]
    xc_ref[...] = xc
    for s in range(N_SLABS):
        ls = slice(LANES * s, LANES * (s + 1))
        xcs = xc_ref[:, ls]
        ri = _dot(xcs.astype(BF16), wlru_ref[s])
        r = _sigmoid(ri[:, :LANES] + ba_ref[:, ls])
        i_gate = _sigmoid(ri[:, LANES:] + bx_ref[:, ls])
        log_a = -LRU_C * r * jax.nn.softplus(-lam_ref[:, ls])
        a = jnp.exp(log_a)
        la_ref[:, ls] = a
        lg_ref[:, ls] = xcs * i_gate * jnp.sqrt(-jnp.tanh(log_a) * (a * a + 1.0))

    def lru_body(blk, carry):
        b0 = pl.multiple_of(blk * SUBLANES, SUBLANES)
        a = la_ref[pl.ds(b0, SUBLANES), :]
        b = lg_ref[pl.ds(b0, SUBLANES), :]
        for d in (1, 2, 4):
            keep = row_id >= d
            sa = jnp.where(keep, pltpu.roll(a, d, 0), 1.0)
            sb = jnp.where(keep, pltpu.roll(b, d, 0), 0.0)
            b = b + a * sb
            a = a * sa
        hblk = b + a * carry
        lg_ref[pl.ds(b0, SUBLANES), :] = hblk
        return hblk[SUBLANES - 1:SUBLANES, :]

    lcar_ref[...] = lax.fori_loop(0, n_blocks, lru_body, lcar_ref[...])
    ocat_ref[:, 2 * WIDTH:3 * WIDTH] = (lg_ref[...] * _silu(proj_ref[:, C_LG:C_LG + WIDTH])).astype(BF16)

    half = S5_SLAB_STATE
    for s in range(N_SLABS):
        ls = slice(LANES * s, LANES * (s + 1))
        us = proj_ref[:, C_SU + LANES * s:C_SU + LANES * (s + 1)]
        xs_ref[...] = _dot(us.astype(BF16), wb_ref[s])

        def s5_body(blk, carry, s=s):
            cr, ci = carry
            b0 = pl.multiple_of(blk * SUBLANES, SUBLANES)
            hr = xs_ref[pl.ds(b0, SUBLANES), 0:half]
            hi = xs_ref[pl.ds(b0, SUBLANES), half:2 * half]
            for k, d in enumerate((1, 2, 4)):
                pr = s5tab_ref[s, 2 * k]
                pi = s5tab_ref[s, 2 * k + 1]
                sr = pltpu.roll(hr, d, 0)
                si = pltpu.roll(hi, d, 0)
                hr, hi = hr + pr * sr - pi * si, hi + pr * si + pi * sr
            pr = s5tab_ref[s, 6]
            pi = s5tab_ref[s, 7]
            hr, hi = hr + pr * cr - pi * ci, hi + pr * ci + pi * cr
            xs_ref[pl.ds(b0, SUBLANES), 0:half] = hr
            xs_ref[pl.ds(b0, SUBLANES), half:2 * half] = hi
            return hr[SUBLANES - 1:SUBLANES, :], hi[SUBLANES - 1:SUBLANES, :]

        cr, ci = lax.fori_loop(0, n_blocks, s5_body, (scar_ref[s, 0:1, :], scar_ref[s, 1:2, :]))
        scar_ref[s, 0:1, :] = cr
        scar_ref[s, 1:2, :] = ci
        y = _dot(xs_ref[...].astype(BF16), wc_ref[s]) + s5d_ref[:, ls] * us
        y5_ref[:, ls] = _gelu_tanh(y)
    y5 = y5_ref[...]
    glu = _sigmoid(_dot(y5.astype(BF16), gluw_ref[...]) + glub_ref[...])
    ocat_ref[:, 3 * WIDTH:4 * WIDTH] = (y5 * glu * _silu(proj_ref[:, C_SG:C_SG + WIDTH])).astype(BF16)


MERGE_COLS = 256


def _merge_kernel(h_ref, ocat_ref, ng_ref, wg_ref, wbr_ref, wout_ref, fg_ref, out_ref, ub_ref, m_ref,
                  *, final_norm):
    x = h_ref[...]
    ub_ref[...] = _rmsnorm_rows(x, ng_ref[...]).astype(BF16)
    for nb in range(D_MODEL // MERGE_COLS):
        cols = slice(MERGE_COLS * nb, MERGE_COLS * (nb + 1))
        merged = None
        for bi in range(N_BRANCHES):
            gate = _sigmoid(_dot(ub_ref[...], wg_ref[bi, :, cols]))
            branch = _dot(ocat_ref[:, WIDTH * bi:WIDTH * (bi + 1)], wbr_ref[bi, :, cols])
            term = gate * branch
            merged = term if merged is None else merged + term
        m_ref[:, cols] = merged.astype(BF16)
    out = x + _dot(m_ref[...], wout_ref[...])
    if final_norm:
        out = _rmsnorm_rows(out, fg_ref[...])
    out_ref[...] = out


def _const_spec(shape):
    nd = len(shape)
    return pl.BlockSpec(shape, lambda b, j, _nd=nd: (0,) * _nd, pipeline_mode=pl.Buffered(1))


def _block_diag(blocks):
    n, r, c = blocks.shape
    eye = jnp.eye(n, dtype=blocks.dtype)
    return (blocks[:, :, None, :] * eye[:, None, :, None]).reshape(n * r, n * c)


def _attention_constants():
    pos = jnp.arange(SUB)
    same_chunk = (pos[:, None] // CHUNK) == (pos[None, :] // CHUNK)
    causal = same_chunk & (pos[:, None] >= pos[None, :])
    tri = causal.astype(BF16)
    bones = same_chunk.astype(BF16)
    cmask = causal.astype(F32)
    log_gamma = jnp.log1p(-jnp.exp2(-5.0 - jnp.arange(HEADS, dtype=F32)))
    rel = (pos[:, None] - pos[None, :]).astype(F32)
    decay = jnp.exp(jnp.maximum(rel, 0.0)[None] * log_gamma[:, None, None])
    rmask = jnp.where(causal[None], decay, 0.0).astype(F32)
    cpos = (pos % CHUNK).astype(F32)
    k_w = jnp.exp((CHUNK - 1.0 - cpos)[:, None] * log_gamma[None, :])
    q_w = jnp.exp((cpos + 1.0)[:, None] * log_gamma[None, :])
    rkw = jnp.repeat(k_w, DK, axis=1)
    rqw = jnp.repeat(q_w, DK, axis=1)
    rdec = jnp.repeat(jnp.exp(CHUNK * log_gamma), DK)[None, :]
    return tri, bones, cmask, rmask, rkw, rqw, rdec


def _rope_tables(seq):
    half = DK // 2
    inv = ROPE_BASE ** (-jnp.arange(half, dtype=F32) / half)
    ang = jnp.arange(seq, dtype=F32)[:, None] * inv[None, :]
    cos = jnp.cos(ang)
    sin = jnp.sin(ang)
    cos_t = jnp.concatenate([cos, cos] * (LANES // DK), axis=1)
    sin_t = jnp.concatenate([-sin, sin] * (LANES // DK), axis=1)
    return cos_t, sin_t


def _s5_params(a_re, a_im, b_re, b_im, c_re, c_im, log_dt):
    step = jnp.exp(log_dt)[:, None]
    mag = jnp.exp(step * a_re)
    ab_re = mag * jnp.cos(step * a_im)
    ab_im = mag * jnp.sin(step * a_im)
    den = a_re * a_re + a_im * a_im
    f_re = ((ab_re - 1.0) * a_re + ab_im * a_im) / den
    f_im = (ab_im * a_re - (ab_re - 1.0) * a_im) / den
    bb_re = f_re[..., None] * b_re - f_im[..., None] * b_im
    bb_im = f_re[..., None] * b_im + f_im[..., None] * b_re
    gps = LANES // S5_GROUP
    wb, wc = [], []
    for s in range(N_SLABS):
        g = slice(gps * s, gps * (s + 1))
        wb.append(jnp.concatenate([_block_diag(jnp.swapaxes(bb_re[g], 1, 2)),
                                   _block_diag(jnp.swapaxes(bb_im[g], 1, 2))], axis=1))
        wc.append(jnp.concatenate([_block_diag(jnp.swapaxes(c_re[g], 1, 2)),
                                   _block_diag(jnp.swapaxes(-c_im[g], 1, 2))], axis=0))
    wb = jnp.stack(wb).astype(BF16)
    wc = jnp.stack(wc).astype(BF16)
    lam_r = ab_re.reshape(N_SLABS, S5_SLAB_STATE)
    lam_i = ab_im.reshape(N_SLABS, S5_SLAB_STATE)
    pows_r, pows_i = [lam_r], [lam_i]
    for _ in range(SUBLANES - 1):
        pr, pi = pows_r[-1], pows_i[-1]
        pows_r.append(pr * lam_r - pi * lam_i)
        pows_i.append(pr * lam_i + pi * lam_r)
    sub = jnp.arange(SUBLANES)[None, :, None]
    tabs = []
    for d in (1, 2, 4):
        keep = sub >= d
        tabs.append(jnp.where(keep, pows_r[d - 1][:, None, :], 0.0))
        tabs.append(jnp.where(keep, pows_i[d - 1][:, None, :], 0.0))
    tabs.append(jnp.stack(pows_r, axis=1))
    tabs.append(jnp.stack(pows_i, axis=1))
    tabs = jnp.stack(tabs, axis=1).astype(F32)
    return wb, wc, tabs


def _lru_weights(w_a, w_x):
    per = LANES // LRU_BLOCK_DIM
    out = []
    for s in range(N_SLABS):
        blk = slice(per * s, per * (s + 1))
        out.append(jnp.concatenate([_block_diag(w_a[blk]), _block_diag(w_x[blk])], axis=1))
    return jnp.stack(out).astype(BF16)


def _repack_w_in(w):
    pad = jnp.zeros((D_MODEL, LANES - GLA_RANK), w.dtype)
    lr0 = 2 * KEY + WIDTH
    return jnp.concatenate([w[:, :lr0], w[:, lr0 + GLA_RANK:], w[:, lr0:lr0 + GLA_RANK], pad], axis=1).astype(BF16)


def _mixers_call(h2, batch, seq, p, consts):
    tile = min(TILE, seq)
    nt = seq // tile
    row_spec = lambda cols: pl.BlockSpec((tile, cols), lambda b, j: (b * nt + j, 0))
    time_spec = pl.BlockSpec((tile, LANES), lambda b, j: (j, 0))
    tri, bones, cmask, rmask, rkw, rqw, rdec, cos_t, sin_t = consts
    operands = [
        (h2, row_spec(D_MODEL)),
        (p['norm_gain'], None), (p['w_in'], None), (p['w_lr'], None), (p['b_lr'], None),
        (p['gla_gain'], None), (p['ret_gain'], None), (p['ret_bias'], None),
        (cos_t, time_spec), (sin_t, time_spec),
        (tri, None), (bones, None), (cmask, None), (rmask, None), (rkw, None), (rqw, None), (rdec, None),
        (p['conv_w'], None), (p['conv_b'], None), (p['w_lru'], None), (p['b_a'], None), (p['b_x'], None),
        (p['lam'], None),
        (p['s5_wb'], None), (p['s5_wc'], None), (p['s5_tab'], None), (p['s5_d'], None),
        (p['glu_w'], None), (p['glu_b'], None),
    ]
    args = [a for a, _ in operands]
    specs = [s if s is not None else _const_spec(a.shape) for a, s in operands]
    scratch = [
        pltpu.VMEM((tile, D_MODEL), BF16),
        pltpu.VMEM((tile, N_COLS), F32),
        pltpu.VMEM((HEADS, DV, LANES), F32),
        pltpu.VMEM((HEADS, DV, LANES), F32),
        pltpu.VMEM((tile + 2 * SUBLANES, WIDTH), F32),
        pltpu.VMEM((tile, WIDTH), F32),
        pltpu.VMEM((tile, WIDTH), F32),
        pltpu.VMEM((tile, WIDTH), F32),
        pltpu.VMEM((1, WIDTH), F32),
        pltpu.VMEM((tile, 2 * S5_SLAB_STATE), F32),
        pltpu.VMEM((N_SLABS, 2, S5_SLAB_STATE), F32),
        pltpu.VMEM((tile, WIDTH), F32),
    ]
    return pl.pallas_call(
        _mixers_kernel,
        grid=(batch, nt),
        in_specs=specs,
        out_specs=row_spec(N_BRANCHES * WIDTH),
        out_shape=jax.ShapeDtypeStruct((batch * seq, N_BRANCHES * WIDTH), BF16),
        scratch_shapes=scratch,
        compiler_params=pltpu.CompilerParams(dimension_semantics=("arbitrary", "arbitrary"),
                                             vmem_limit_bytes=VMEM_LIMIT_BYTES),
        name="mixers",
    )(*args)


def _merge_call(h2, ocat, batch, seq, p, final_gain, final_norm):
    tile = min(TILE, seq)
    nt = seq // tile
    row_spec = lambda cols: pl.BlockSpec((tile, cols), lambda b, j: (b * nt + j, 0))
    args = [h2, ocat, p['norm_gain'], p['w_gate'], p['w_branch'], p['w_out'], final_gain]
    specs = [row_spec(D_MODEL), row_spec(N_BRANCHES * WIDTH)] + [_const_spec(a.shape) for a in args[2:]]
    return pl.pallas_call(
        functools.partial(_merge_kernel, final_norm=final_norm),
        grid=(batch, nt),
        in_specs=specs,
        out_specs=row_spec(D_MODEL),
        out_shape=jax.ShapeDtypeStruct((batch * seq, D_MODEL), F32),
        scratch_shapes=[pltpu.VMEM((tile, D_MODEL), BF16), pltpu.VMEM((tile, D_MODEL), BF16)],
        compiler_params=pltpu.CompilerParams(dimension_semantics=("arbitrary", "arbitrary"),
                                             vmem_limit_bytes=VMEM_LIMIT_BYTES),
        name="merge",
    )(*args)


@jax.jit
def _forward(x, norm_gain, w_in, gla_w_lr, gla_b_lr, gla_norm_gain, ret_norm_gain, ret_norm_bias,
             lru_conv_w, lru_conv_b, lru_w_a, lru_b_a, lru_w_x, lru_b_x, lru_lambda,
             s5_a_re, s5_a_im, s5_b_re, s5_b_im, s5_c_re, s5_c_im, s5_d, s5_log_dt, s5_glu_w, s5_glu_b,
             w_merge_gate, w_branch, w_out, final_norm_gain):
    batch, seq, _ = x.shape
    depth = w_in.shape[0]
    assert seq % SUB == 0 and seq % min(TILE, seq) == 0
    consts = _attention_constants() + _rope_tables(seq)
    row = lambda v: v.astype(F32)[None, :]
    h2 = x.astype(F32).reshape(batch * seq, D_MODEL)
    for l in range(depth):
        s5_wb, s5_wc, s5_tab = _s5_params(s5_a_re[l], s5_a_im[l], s5_b_re[l], s5_b_im[l],
                                          s5_c_re[l], s5_c_im[l], s5_log_dt[l])
        p = {
            'norm_gain': row(norm_gain[l]),
            'w_in': _repack_w_in(w_in[l]),
            'w_lr': jnp.concatenate([gla_w_lr[l], jnp.zeros((LANES - GLA_RANK, KEY), F32)], axis=0).astype(BF16),
            'b_lr': row(gla_b_lr[l]),
            'gla_gain': row(gla_norm_gain[l]),
            'ret_gain': row(ret_norm_gain[l]),
            'ret_bias': row(ret_norm_bias[l]),
            'conv_w': lru_conv_w[l].astype(F32),
            'conv_b': row(lru_conv_b[l]),
            'w_lru': _lru_weights(lru_w_a[l], lru_w_x[l]),
            'b_a': row(lru_b_a[l]),
            'b_x': row(lru_b_x[l]),
            'lam': row(lru_lambda[l]),
            's5_wb': s5_wb, 's5_wc': s5_wc, 's5_tab': s5_tab,
            's5_d': row(s5_d[l]),
            'glu_w': s5_glu_w[l].astype(BF16),
            'glu_b': row(s5_glu_b[l]),
            'w_gate': w_merge_gate[l].astype(BF16),
            'w_branch': w_branch[l].astype(BF16),
            'w_out': w_out[l].astype(BF16),
        }
        ocat = _mixers_call(h2, batch, seq, p, consts)
        h2 = _merge_call(h2, ocat, batch, seq, p, row(final_norm_gain), final_norm=(l == depth - 1))
    return h2.reshape(batch, seq, D_MODEL).astype(x.dtype)


def kernel(x, norm_gain, w_in, gla_w_lr, gla_b_lr, gla_norm_gain, ret_norm_gain, ret_norm_bias, lru_conv_w,
           lru_conv_b, lru_w_a, lru_b_a, lru_w_x, lru_b_x, lru_lambda, s5_a_re, s5_a_im, s5_b_re, s5_b_im,
           s5_c_re, s5_c_im, s5_d, s5_log_dt, s5_glu_w, s5_glu_b, w_merge_gate, w_branch, w_out,
           final_norm_gain):
    return _forward(x, norm_gain, w_in, gla_w_lr, gla_b_lr, gla_norm_gain, ret_norm_gain, ret_norm_bias,
                    lru_conv_w, lru_conv_b, lru_w_a, lru_b_a, lru_w_x, lru_b_x, lru_lambda,
                    s5_a_re, s5_a_im, s5_b_re, s5_b_im, s5_c_re, s5_c_im, s5_d, s5_log_dt, s5_glu_w, s5_glu_b,
                    w_merge_gate, w_branch, w_out, final_norm_gain)
```

```python
import functools
import math

import jax
import jax.numpy as jnp
from jax import lax
from jax.experimental import pallas as pl
from jax.experimental.pallas import tpu as pltpu

F32 = jnp.float32
BF16 = jnp.bfloat16

D_MODEL = 1024
N_BRANCHES = 4
WIDTH = 512
NORM_EPS = 1e-6
CHUNK = 64
HEADS = 4
DK = 64
DV = WIDTH // HEADS
KEY = HEADS * DK
GLA_RANK = 16
GLA_GATE_TEMP = 16.0
ROPE_BASE = 10000.0
LRU_BLOCKS = 8
LRU_BLOCK_DIM = WIDTH // LRU_BLOCKS
CONV_WIDTH = 4
LRU_C = 8.0
S5_GROUP = 16
S5_GROUPS = WIDTH // S5_GROUP
S5_STATE = 64

LANES = 128
SUBLANES = 8
VMEM_LIMIT_BYTES = 60 * 1024 * 1024

GROUP = SUBLANES
ROWS = GROUP * CHUNK
SUB = 256
SUB_CHUNKS = SUB // CHUNK
N_SUB = ROWS // SUB

C_GQ, C_GK, C_GV, C_GG = 0, 256, 512, 1024
C_RQ, C_RK, C_RV, C_RG = 1536, 1792, 2048, 2560
C_LR = 3072
N_NAT = C_LR + LANES
T_LX, T_LG, T_SU, T_SG = 0, 512, 1024, 1536
N_TM = 2048
PROJ_BLOCK = 512
N_SLABS = WIDTH // LANES
S5_SLAB_STATE = (LANES // S5_GROUP) * S5_STATE
CONV_HEAD = (CONV_WIDTH - 1) * GROUP

MERGE_TILE = 512
MERGE_COLS = 256


def _sigmoid(x):
    return jax.nn.sigmoid(x)


def _silu(x):
    return x * jax.nn.sigmoid(x)


def _gelu_tanh(x):
    c = math.sqrt(2.0 / math.pi)
    return 0.5 * x * (1.0 + jnp.tanh(c * (x + 0.044715 * (x * x * x))))


def _dot(a, b):
    return jnp.dot(a, b, preferred_element_type=F32)


def _dot_nt(a, b):
    return lax.dot_general(a, b, (((1,), (1,)), ((), ())), preferred_element_type=F32)


def _dot_tn(a, b):
    return lax.dot_general(a, b, (((0,), (0,)), ((), ())), preferred_element_type=F32)


def _rmsnorm_rows(x, gain):
    return x * lax.rsqrt(jnp.mean(x * x, axis=-1, keepdims=True) + NORM_EPS) * gain


def _split_bf16(x):
    hi = x.astype(BF16)
    lo = (x - hi.astype(F32)).astype(BF16)
    return hi, lo


def _lane_mean(x, avg):
    hi, lo = _split_bf16(x)
    return _dot(hi, avg) + _dot(lo, avg)


def _mixers_kernel(h_ref, ng_ref, wnat_ref, wtm_ref, pt_ref, p_ref, wlr_ref, blr_ref, gng_ref, rng_ref, rnb_ref,
                   cos_ref, sin_ref, tri_ref, bones_ref, cmask_ref, avg_ref, rmask_ref, rkw_ref, rqw_ref, rdec_ref,
                   convw_ref, convb_ref, wlru_ref, ba_ref, bx_ref, lam_ref,
                   wb_ref, wc_ref, s5lam_ref, s5d_ref, gluw_ref, glub_ref,
                   ocat_ref,
                   ub_ref, ut_ref, proj_ref, projt_ref, gst_ref, rst_ref, lxbuf_ref, la_ref, lg_ref, lcar_ref,
                   xs_ref, hs_ref, scar_ref, otm_ref):
    j = pl.program_id(1)

    @pl.when(j == 0)
    def _():
        gst_ref[...] = jnp.zeros_like(gst_ref)
        rst_ref[...] = jnp.zeros_like(rst_ref)
        lxbuf_ref[0:CONV_HEAD, :] = jnp.zeros((CONV_HEAD, WIDTH), F32)
        lcar_ref[...] = jnp.zeros_like(lcar_ref)
        scar_ref[...] = jnp.zeros_like(scar_ref)

    @pl.when(j != 0)
    def _():
        lxbuf_ref[0:CONV_HEAD, :] = lxbuf_ref[ROWS:ROWS + CONV_HEAD, :]

    x = h_ref[...].reshape(ROWS, D_MODEL)
    ub_ref[...] = _rmsnorm_rows(x, ng_ref[...]).astype(BF16)
    ut_ref[...] = _dot(pt_ref[...], ub_ref[...]).astype(BF16)

    def nat_body(n, carry):
        c0 = pl.multiple_of(n * PROJ_BLOCK, PROJ_BLOCK)
        proj_ref[:, pl.ds(c0, PROJ_BLOCK)] = _dot(ub_ref[...], wnat_ref[:, pl.ds(c0, PROJ_BLOCK)])
        return carry

    lax.fori_loop(0, C_LR // PROJ_BLOCK, nat_body, 0)
    proj_ref[:, C_LR:N_NAT] = _dot(ub_ref[...], wnat_ref[:, C_LR:N_NAT])

    def tm_body(n, carry):
        c0 = pl.multiple_of(n * PROJ_BLOCK, PROJ_BLOCK)
        projt_ref[:, pl.ds(c0, PROJ_BLOCK)] = _dot(ut_ref[...], wtm_ref[:, pl.ds(c0, PROJ_BLOCK)])
        return carry

    lax.fori_loop(0, N_TM // PROJ_BLOCK, tm_body, 0)

    lane = lax.broadcasted_iota(jnp.int32, (1, LANES), 1)
    head_masks = (lane < DK, lane >= DK)
    cmask = cmask_ref[...] != 0.0
    avg = avg_ref[...]
    lane_k = lax.broadcasted_iota(jnp.int32, (1, KEY), 1)
    first_half = (lane_k % DK) < (DK // 2)
    cos_t = jnp.concatenate([cos_ref[...]] * SUB_CHUNKS, axis=0)
    cos_t = jnp.concatenate([cos_t] * (KEY // LANES), axis=1)
    sin_t = jnp.concatenate([sin_ref[...]] * SUB_CHUNKS, axis=0)
    sin_t = jnp.concatenate([sin_t] * (KEY // LANES), axis=1)

    def rope(t):
        swapped = jnp.where(first_half, pltpu.roll(t, KEY - DK // 2, 1), pltpu.roll(t, DK // 2, 1))
        return t * cos_t + swapped * sin_t

    for st in range(N_SUB):
        rows = slice(st * SUB, (st + 1) * SUB)

        lr = proj_ref[rows, C_LR:C_LR + LANES].astype(BF16)
        z = _dot(lr, wlr_ref[...]) + blr_ref[...]
        log_a = jax.nn.log_sigmoid(z) * (1.0 / GLA_GATE_TEMP)
        la_hl = jnp.concatenate(_split_bf16(log_a), axis=1)
        cs = _dot(tri_ref[...], la_hl)
        cum = cs[:, :KEY] + cs[:, KEY:]
        ts = _dot(bones_ref[...], la_hl)
        tot = ts[:, :KEY] + ts[:, KEY:]
        gq = proj_ref[rows, C_GQ:C_GQ + KEY]
        gk = proj_ref[rows, C_GK:C_GK + KEY]
        q_dec = gq * (DK ** -0.5) * jnp.exp(cum)
        k_dec = gk * jnp.exp(-cum)
        k_tail = gk * jnp.exp(tot - cum)
        e_dec = jnp.exp(tot)

        for hd in range(HEADS):
            pl0 = LANES * (hd // 2)
            hm = head_masks[hd % 2]
            sidx = hd * N_SUB + st
            qm = jnp.where(hm, q_dec[:, pl0:pl0 + LANES], 0.0).astype(BF16)
            kd = k_dec[:, pl0:pl0 + LANES].astype(BF16)
            kt = k_tail[:, pl0:pl0 + LANES].astype(BF16)
            vh = proj_ref[rows, C_GV + DV * hd:C_GV + DV * (hd + 1)].astype(BF16)
            sc = jnp.where(cmask, _dot_nt(qm, kd), 0.0)
            o = _dot(sc.astype(BF16), vh)
            inter = _dot_nt(qm, gst_ref[sidx].astype(BF16))
            outs = []
            for c in range(SUB_CHUNKS):
                cr = slice(CHUNK * c, CHUNK * (c + 1))
                sr = slice(DV * c, DV * (c + 1))
                outs.append(o[cr] + inter[cr, sr])
                kv_t = _dot_tn(vh[cr], kt[cr])
                gst_ref[sidx, sr, :] = gst_ref[sidx, sr, :] * e_dec[CHUNK * c:CHUNK * c + 1, pl0:pl0 + LANES] + kv_t
            oh = jnp.concatenate(outs, axis=0)
            oh = oh * lax.rsqrt(_lane_mean(oh * oh, avg) + NORM_EPS)
            oh = oh * gng_ref[:, DV * hd:DV * (hd + 1)]
            gate = proj_ref[rows, C_GG + DV * hd:C_GG + DV * (hd + 1)]
            ocat_ref[GROUP // N_SUB * st:GROUP // N_SUB * (st + 1), :, DV * hd:DV * (hd + 1)] = (
                (oh * _silu(gate)).astype(BF16).reshape(GROUP // N_SUB, CHUNK, DV))

        qr = rope(proj_ref[rows, C_RQ:C_RQ + KEY])
        kr = rope(proj_ref[rows, C_RK:C_RK + KEY]) * (DK ** -0.5)
        q_w = qr * rqw_ref[...]
        k_w = kr * rkw_ref[...]
        for hd in range(HEADS):
            pl0 = LANES * (hd // 2)
            hm = head_masks[hd % 2]
            sidx = hd * N_SUB + st
            qm = jnp.where(hm, qr[:, pl0:pl0 + LANES], 0.0).astype(BF16)
            qwm = jnp.where(hm, q_w[:, pl0:pl0 + LANES], 0.0).astype(BF16)
            kd = kr[:, pl0:pl0 + LANES].astype(BF16)
            kt = k_w[:, pl0:pl0 + LANES].astype(BF16)
            vh = proj_ref[rows, C_RV + DV * hd:C_RV + DV * (hd + 1)].astype(BF16)
            sc = _dot_nt(qm, kd) * rmask_ref[hd]
            o = _dot(sc.astype(BF16), vh)
            inter = _dot_nt(qwm, rst_ref[sidx].astype(BF16))
            outs = []
            for c in range(SUB_CHUNKS):
                cr = slice(CHUNK * c, CHUNK * (c + 1))
                sr = slice(DV * c, DV * (c + 1))
                outs.append(o[cr] + inter[cr, sr])
                kv_t = _dot_tn(vh[cr], kt[cr])
                rst_ref[sidx, sr, :] = rst_ref[sidx, sr, :] * rdec_ref[:, pl0:pl0 + LANES] + kv_t
            oh = jnp.concatenate(outs, axis=0)
            cen = oh - _lane_mean(oh, avg)
            oh = cen * lax.rsqrt(_lane_mean(cen * cen, avg) + NORM_EPS)
            oh = oh * rng_ref[:, DV * hd:DV * (hd + 1)] + rnb_ref[:, DV * hd:DV * (hd + 1)]
            gate = proj_ref[rows, C_RG + DV * hd:C_RG + DV * (hd + 1)]
            ocat_ref[GROUP // N_SUB * st:GROUP // N_SUB * (st + 1), :, WIDTH + DV * hd:WIDTH + DV * (hd + 1)] = (
                (oh * _silu(gate)).astype(BF16).reshape(GROUP // N_SUB, CHUNK, DV))

    lxbuf_ref[CONV_HEAD:CONV_HEAD + ROWS, :] = projt_ref[:, T_LX:T_LX + WIDTH]
    for s in range(N_SLABS):
        ls = slice(LANES * s, LANES * (s + 1))
        xc = convb_ref[:, ls] + convw_ref[0:1, ls] * lxbuf_ref[0:ROWS, ls]
        for tap in range(1, CONV_WIDTH):
            xc = xc + convw_ref[tap:tap + 1, ls] * lxbuf_ref[GROUP * tap:GROUP * tap + ROWS, ls]
        ri = _dot(xc.astype(BF16), wlru_ref[s])
        r = _sigmoid(ri[:, :LANES] + ba_ref[:, ls])
        i_gate = _sigmoid(ri[:, LANES:] + bx_ref[:, ls])
        log_a = -LRU_C * r * jax.nn.softplus(-lam_ref[:, ls])
        a = jnp.exp(log_a)
        la_ref[:, ls] = a
        lg_ref[:, ls] = xc * i_gate * jnp.sqrt(-jnp.tanh(log_a) * (a * a + 1.0))

    def lru_body(t, hprev):
        b0 = pl.multiple_of(t * GROUP, GROUP)
        hnew = la_ref[pl.ds(b0, GROUP), :] * hprev + lg_ref[pl.ds(b0, GROUP), :]
        lg_ref[pl.ds(b0, GROUP), :] = hnew
        return hnew

    lcar_ref[...] = lax.fori_loop(0, CHUNK, lru_body, lcar_ref[...], unroll=4)
    otm_ref[:, 0:WIDTH] = (lg_ref[...] * _silu(projt_ref[:, T_LG:T_LG + WIDTH])).astype(BF16)

    half = S5_SLAB_STATE
    for s in range(N_SLABS):
        ls = slice(LANES * s, LANES * (s + 1))
        us = projt_ref[:, T_SU + LANES * s:T_SU + LANES * (s + 1)]
        xs_ref[...] = _dot(us.astype(BF16), wb_ref[s])
        lam_r = s5lam_ref[s, 0]
        lam_i = s5lam_ref[s, 1]

        def s5_body(t2, carry):
            hr, hi = carry
            b0 = pl.multiple_of(t2 * 2 * GROUP, 2 * GROUP)
            out_r, out_i = [], []
            for k in range(2):
                xr = xs_ref[pl.ds(b0 + GROUP * k, GROUP), 0:half]
                xi = xs_ref[pl.ds(b0 + GROUP * k, GROUP), half:2 * half]
                hr, hi = lam_r * hr - lam_i * hi + xr, lam_r * hi + lam_i * hr + xi
                out_r.append(hr)
                out_i.append(hi)
            hs_ref[pl.ds(b0, 2 * GROUP), 0:half] = jnp.concatenate(out_r, axis=0).astype(BF16)
            hs_ref[pl.ds(b0, 2 * GROUP), half:2 * half] = jnp.concatenate(out_i, axis=0).astype(BF16)
            return hr, hi

        cr, ci = lax.fori_loop(0, CHUNK // 2, s5_body, (scar_ref[s, 0], scar_ref[s, 1]), unroll=2)
        scar_ref[s, 0] = cr
        scar_ref[s, 1] = ci
        y = _dot(hs_ref[...], wc_ref[s]) + s5d_ref[:, ls] * us
        lg_ref[:, ls] = _gelu_tanh(y)
    y5 = lg_ref[...]
    glu = _sigmoid(_dot(y5.astype(BF16), gluw_ref[...]) + glub_ref[...])
    otm_ref[:, WIDTH:2 * WIDTH] = (y5 * glu * _silu(projt_ref[:, T_SG:T_SG + WIDTH])).astype(BF16)

    onat = _dot(p_ref[...], otm_ref[...]).astype(BF16)
    ocat_ref[:, :, 2 * WIDTH:4 * WIDTH] = onat.reshape(GROUP, CHUNK, 2 * WIDTH)


def _merge_kernel(h_ref, ocat_ref, ng_ref, wg_ref, wbr_ref, wout_ref, fg_ref, out_ref, ub_ref, m_ref,
                  *, final_norm):
    x = h_ref[...]
    ub_ref[...] = _rmsnorm_rows(x, ng_ref[...]).astype(BF16)
    for nb in range(D_MODEL // MERGE_COLS):
        cols = slice(MERGE_COLS * nb, MERGE_COLS * (nb + 1))
        merged = None
        for bi in range(N_BRANCHES):
            gate = _sigmoid(_dot(ub_ref[...], wg_ref[bi, :, cols]))
            branch = _dot(ocat_ref[:, WIDTH * bi:WIDTH * (bi + 1)], wbr_ref[bi, :, cols])
            term = gate * branch
            merged = term if merged is None else merged + term
        m_ref[:, cols] = merged.astype(BF16)
    out = x + _dot(m_ref[...], wout_ref[...])
    if final_norm:
        out = _rmsnorm_rows(out, fg_ref[...])
    out_ref[...] = out


def _const_spec(shape):
    nd = len(shape)
    return pl.BlockSpec(shape, lambda b, j, _nd=nd: (0,) * _nd, pipeline_mode=pl.Buffered(1))


def _block_diag(blocks):
    n, r, c = blocks.shape
    eye = jnp.eye(n, dtype=blocks.dtype)
    return (blocks[:, :, None, :] * eye[:, None, :, None]).reshape(n * r, n * c)


def _tile_constants():
    pos = jnp.arange(SUB)
    same_chunk = (pos[:, None] // CHUNK) == (pos[None, :] // CHUNK)
    causal = same_chunk & (pos[:, None] >= pos[None, :])
    tri = causal.astype(BF16)
    bones = same_chunk.astype(BF16)
    cmask = causal.astype(F32)
    avg = jnp.full((LANES, LANES), 1.0 / LANES, BF16)
    log_gamma = jnp.log1p(-jnp.exp2(-5.0 - jnp.arange(HEADS, dtype=F32)))
    rel = (pos[:, None] - pos[None, :]).astype(F32)
    decay = jnp.exp(jnp.maximum(rel, 0.0)[None] * log_gamma[:, None, None])
    rmask = jnp.where(causal[None], decay, 0.0).astype(F32)
    cpos = (pos % CHUNK).astype(F32)
    k_w = jnp.exp((CHUNK - 1.0 - cpos)[:, None] * log_gamma[None, :])
    q_w = jnp.exp((cpos + 1.0)[:, None] * log_gamma[None, :])
    rkw = jnp.repeat(k_w, DK, axis=1)
    rqw = jnp.repeat(q_w, DK, axis=1)
    rdec = jnp.repeat(jnp.exp(CHUNK * log_gamma), DK)[None, :]
    r_tm = jnp.arange(ROWS)
    src = (r_tm % GROUP) * CHUNK + r_tm // GROUP
    pt = (src[:, None] == jnp.arange(ROWS)[None, :]).astype(BF16)
    return dict(tri=tri, bones=bones, cmask=cmask, avg=avg, rmask=rmask, rkw=rkw, rqw=rqw, rdec=rdec,
                pt=pt, p=pt.T)


def _rope_tables(seq):
    half = DK // 2
    inv = ROPE_BASE ** (-jnp.arange(half, dtype=F32) / half)
    ang = jnp.arange(seq, dtype=F32)[:, None] * inv[None, :]
    cos = jnp.cos(ang)
    sin = jnp.sin(ang)
    cos_t = jnp.concatenate([cos, cos] * (LANES // DK), axis=1)
    sin_t = jnp.concatenate([-sin, sin] * (LANES // DK), axis=1)
    return cos_t, sin_t


def _s5_params(a_re, a_im, b_re, b_im, c_re, c_im, log_dt):
    step = jnp.exp(log_dt)[:, None]
    mag = jnp.exp(step * a_re)
    ab_re = mag * jnp.cos(step * a_im)
    ab_im = mag * jnp.sin(step * a_im)
    den = a_re * a_re + a_im * a_im
    f_re = ((ab_re - 1.0) * a_re + ab_im * a_im) / den
    f_im = (ab_im * a_re - (ab_re - 1.0) * a_im) / den
    bb_re = f_re[..., None] * b_re - f_im[..., None] * b_im
    bb_im = f_re[..., None] * b_im + f_im[..., None] * b_re
    gps = LANES // S5_GROUP
    wb, wc = [], []
    for s in range(N_SLABS):
        g = slice(gps * s, gps * (s + 1))
        wb.append(jnp.concatenate([_block_diag(jnp.swapaxes(bb_re[g], 1, 2)),
                                   _block_diag(jnp.swapaxes(bb_im[g], 1, 2))], axis=1))
        wc.append(jnp.concatenate([_block_diag(jnp.swapaxes(c_re[g], 1, 2)),
                                   _block_diag(jnp.swapaxes(-c_im[g], 1, 2))], axis=0))
    wb = jnp.stack(wb).astype(BF16)
    wc = jnp.stack(wc).astype(BF16)
    lam = jnp.stack([ab_re.reshape(N_SLABS, S5_SLAB_STATE), ab_im.reshape(N_SLABS, S5_SLAB_STATE)], axis=1)
    lam = jnp.broadcast_to(lam[:, :, None, :], (N_SLABS, 2, GROUP, S5_SLAB_STATE)).astype(F32)
    return wb, wc, lam


def _lru_weights(w_a, w_x):
    per = LANES // LRU_BLOCK_DIM
    out = []
    for s in range(N_SLABS):
        blk = slice(per * s, per * (s + 1))
        out.append(jnp.concatenate([_block_diag(w_a[blk]), _block_diag(w_x[blk])], axis=1))
    return jnp.stack(out).astype(BF16)


def _repack_w_in(w):
    lr0 = 2 * KEY + WIDTH
    tm0 = lr0 + GLA_RANK + 2 * WIDTH + 2 * KEY + WIDTH
    pad = jnp.zeros((D_MODEL, LANES - GLA_RANK), w.dtype)
    w_nat = jnp.concatenate([w[:, :lr0], w[:, lr0 + GLA_RANK:tm0], w[:, lr0:lr0 + GLA_RANK], pad], axis=1)
    return w_nat.astype(BF16), w[:, tm0:].astype(BF16)


def _mixers_call(h3, p, consts, cos_t, sin_t):
    batch, seq, _ = h3.shape
    time_spec = pl.BlockSpec((CHUNK, LANES), lambda g, j: (j, 0))
    c = consts
    operands = [
        (h3, pl.BlockSpec((GROUP, CHUNK, D_MODEL), lambda g, j: (g, j, 0))),
        (p['norm_gain'], None), (p['w_nat'], None), (p['w_tm'], None), (c['pt'], None), (c['p'], None),
        (p['w_lr'], None), (p['b_lr'], None),
        (p['gla_gain'], None), (p['ret_gain'], None), (p['ret_bias'], None),
        (cos_t, time_spec), (sin_t, time_spec),
        (c['tri'], None), (c['bones'], None), (c['cmask'], None), (c['avg'], None), (c['rmask'], None),
        (c['rkw'], None), (c['rqw'], None), (c['rdec'], None),
        (p['conv_w'], None), (p['conv_b'], None), (p['w_lru'], None), (p['b_a'], None), (p['b_x'], None),
        (p['lam'], None),
        (p['s5_wb'], None), (p['s5_wc'], None), (p['s5_lam'], None), (p['s5_d'], None),
        (p['glu_w'], None), (p['glu_b'], None),
    ]
    args = [a for a, _ in operands]
    specs = [s if s is not None else _const_spec(a.shape) for a, s in operands]
    scratch = [
        pltpu.VMEM((ROWS, D_MODEL), BF16),
        pltpu.VMEM((ROWS, D_MODEL), BF16),
        pltpu.VMEM((ROWS, N_NAT), F32),
        pltpu.VMEM((ROWS, N_TM), F32),
        pltpu.VMEM((HEADS * N_SUB, SUB_CHUNKS * DV, LANES), F32),
        pltpu.VMEM((HEADS * N_SUB, SUB_CHUNKS * DV, LANES), F32),
        pltpu.VMEM((ROWS + CONV_HEAD, WIDTH), F32),
        pltpu.VMEM((ROWS, WIDTH), F32),
        pltpu.VMEM((ROWS, WIDTH), F32),
        pltpu.VMEM((GROUP, WIDTH), F32),
        pltpu.VMEM((ROWS, 2 * S5_SLAB_STATE), F32),
        pltpu.VMEM((ROWS, 2 * S5_SLAB_STATE), BF16),
        pltpu.VMEM((N_SLABS, 2, GROUP, S5_SLAB_STATE), F32),
        pltpu.VMEM((ROWS, 2 * WIDTH), BF16),
    ]
    return pl.pallas_call(
        _mixers_kernel,
        grid=(batch // GROUP, seq // CHUNK),
        in_specs=specs,
        out_specs=pl.BlockSpec((GROUP, CHUNK, N_BRANCHES * WIDTH), lambda g, j: (g, j, 0)),
        out_shape=jax.ShapeDtypeStruct((batch, seq, N_BRANCHES * WIDTH), BF16),
        scratch_shapes=scratch,
        compiler_params=pltpu.CompilerParams(dimension_semantics=("arbitrary", "arbitrary"),
                                             vmem_limit_bytes=VMEM_LIMIT_BYTES),
        name="mixers",
    )(*args)


def _merge_call(h2, ocat, p, final_gain, final_norm):
    rows = h2.shape[0]
    tile = min(MERGE_TILE, rows)
    row_spec = lambda cols: pl.BlockSpec((tile, cols), lambda i, j: (i, 0))
    args = [h2, ocat, p['norm_gain'], p['w_gate'], p['w_branch'], p['w_out'], final_gain]
    specs = [row_spec(D_MODEL), row_spec(N_BRANCHES * WIDTH)] + [_const_spec(a.shape) for a in args[2:]]
    return pl.pallas_call(
        functools.partial(_merge_kernel, final_norm=final_norm),
        grid=(rows // tile, 1),
        in_specs=specs,
        out_specs=row_spec(D_MODEL),
        out_shape=jax.ShapeDtypeStruct((rows, D_MODEL), F32),
        scratch_shapes=[pltpu.VMEM((tile, D_MODEL), BF16), pltpu.VMEM((tile, D_MODEL), BF16)],
        compiler_params=pltpu.CompilerParams(dimension_semantics=("arbitrary", "arbitrary"),
                                             vmem_limit_bytes=VMEM_LIMIT_BYTES),
        name="merge",
    )(*args)


@jax.jit
def _forward(x, norm_gain, w_in, gla_w_lr, gla_b_lr, gla_norm_gain, ret_norm_gain, ret_norm_bias,
             lru_conv_w, lru_conv_b, lru_w_a, lru_b_a, lru_w_x, lru_b_x, lru_lambda,
             s5_a_re, s5_a_im, s5_b_re, s5_b_im, s5_c_re, s5_c_im, s5_d, s5_log_dt, s5_glu_w, s5_glu_b,
             w_merge_gate, w_branch, w_out, final_norm_gain):
    batch, seq, _ = x.shape
    depth = w_in.shape[0]
    assert batch % GROUP == 0 and seq % CHUNK == 0 and (batch * seq) % min(MERGE_TILE, batch * seq) == 0
    consts = _tile_constants()
    cos_t, sin_t = _rope_tables(seq)
    row = lambda v: v.astype(F32)[None, :]
    h3 = x.astype(F32)
    for l in range(depth):
        s5_wb, s5_wc, s5_lam = _s5_params(s5_a_re[l], s5_a_im[l], s5_b_re[l], s5_b_im[l],
                                          s5_c_re[l], s5_c_im[l], s5_log_dt[l])
        w_nat, w_tm = _repack_w_in(w_in[l])
        p = {
            'norm_gain': row(norm_gain[l]),
            'w_nat': w_nat, 'w_tm': w_tm,
            'w_lr': jnp.concatenate([gla_w_lr[l], jnp.zeros((LANES - GLA_RANK, KEY), F32)], axis=0).astype(BF16),
            'b_lr': row(gla_b_lr[l]),
            'gla_gain': row(gla_norm_gain[l]),
            'ret_gain': row(ret_norm_gain[l]),
            'ret_bias': row(ret_norm_bias[l]),
            'conv_w': lru_conv_w[l].astype(F32),
            'conv_b': row(lru_conv_b[l]),
            'w_lru': _lru_weights(lru_w_a[l], lru_w_x[l]),
            'b_a': row(lru_b_a[l]),
            'b_x': row(lru_b_x[l]),
            'lam': row(lru_lambda[l]),
            's5_wb': s5_wb, 's5_wc': s5_wc, 's5_lam': s5_lam,
            's5_d': row(s5_d[l]),
            'glu_w': s5_glu_w[l].astype(BF16),
            'glu_b': row(s5_glu_b[l]),
            'w_gate': w_merge_gate[l].astype(BF16),
            'w_branch': w_branch[l].astype(BF16),
            'w_out': w_out[l].astype(BF16),
        }
        ocat = _mixers_call(h3, p, consts, cos_t, sin_t)
        h2 = _merge_call(h3.reshape(batch * seq, D_MODEL), ocat.reshape(batch * seq, N_BRANCHES * WIDTH), p,
                         row(final_norm_gain), final_norm=(l == depth - 1))
        h3 = h2.reshape(batch, seq, D_MODEL)
    return h3.astype(x.dtype)


def kernel(x, norm_gain, w_in, gla_w_lr, gla_b_lr, gla_norm_gain, ret_norm_gain, ret_norm_bias, lru_conv_w,
           lru_conv_b, lru_w_a, lru_b_a, lru_w_x, lru_b_x, lru_lambda, s5_a_re, s5_a_im, s5_b_re, s5_b_im,
           s5_c_re, s5_c_im, s5_d, s5_log_dt, s5_glu_w, s5_glu_b, w_merge_gate, w_branch, w_out,
           final_norm_gain):
    return _forward(x, norm_gain, w_in, gla_w_lr, gla_b_lr, gla_norm_gain, ret_norm_gain, ret_norm_bias,
                    lru_conv_w, lru_conv_b, lru_w_a, lru_b_a, lru_w_x, lru_b_x, lru_lambda,
                    s5_a_re, s5_a_im, s5_b_re, s5_b_im, s5_c_re, s5_c_im, s5_d, s5_log_dt, s5_glu_w, s5_glu_b,
                    w_merge_gate, w_branch, w_out, final_norm_gain)
```

```python
import functools
import math

import jax
import jax.numpy as jnp
from jax import lax
from jax.experimental import pallas as pl
from jax.experimental.pallas import tpu as pltpu

F32 = jnp.float32
BF16 = jnp.bfloat16

D_MODEL = 1024
N_BRANCHES = 4
WIDTH = 512
NORM_EPS = 1e-6
CHUNK = 64
HEADS = 4
DK = 64
DV = WIDTH // HEADS
KEY = HEADS * DK
GLA_RANK = 16
GLA_GATE_TEMP = 16.0
ROPE_BASE = 10000.0
LRU_BLOCKS = 8
LRU_BLOCK_DIM = WIDTH // LRU_BLOCKS
CONV_WIDTH = 4
LRU_C = 8.0
S5_GROUP = 16
S5_GROUPS = WIDTH // S5_GROUP
S5_STATE = 64

LANES = 128
SUBLANES = 8
VMEM_LIMIT_BYTES = 60 * 1024 * 1024

GROUP = SUBLANES
ROWS = GROUP * CHUNK
SUB = 256
SUB_CHUNKS = SUB // CHUNK
N_SUB = ROWS // SUB
PAIRS = HEADS // 2

C_GQ, C_GK, C_GV, C_GG = 0, 256, 512, 1024
C_RQ, C_RK, C_RV, C_RG = 1536, 1792, 2048, 2560
C_LR = 3072
N_NAT = C_LR + LANES
T_LX, T_LG, T_SU, T_SG = 0, 512, 1024, 1536
N_TM = 2048
PROJ_BLOCK = 512
N_SLABS = WIDTH // LANES
S5_SLAB_STATE = (LANES // S5_GROUP) * S5_STATE
CONV_HEAD = (CONV_WIDTH - 1) * GROUP

MERGE_TILE = 512
MERGE_COLS = 256


def _sigmoid(x):
    return jax.nn.sigmoid(x)


def _silu(x):
    return x * jax.nn.sigmoid(x)


def _gelu_tanh(x):
    c = math.sqrt(2.0 / math.pi)
    return 0.5 * x * (1.0 + jnp.tanh(c * (x + 0.044715 * (x * x * x))))


def _dot(a, b):
    return jnp.dot(a, b, preferred_element_type=F32)


def _dot_nt(a, b):
    return lax.dot_general(a, b, (((1,), (1,)), ((), ())), preferred_element_type=F32)


def _dot_tn(a, b):
    return lax.dot_general(a, b, (((0,), (0,)), ((), ())), preferred_element_type=F32)


def _rmsnorm_rows(x, gain):
    return x * lax.rsqrt(jnp.mean(x * x, axis=-1, keepdims=True) + NORM_EPS) * gain


def _split_bf16(x):
    hi = x.astype(BF16)
    lo = (x - hi.astype(F32)).astype(BF16)
    return hi, lo


def _lane_mean(x, avg):
    hi, lo = _split_bf16(x)
    return _dot(hi, avg) + _dot(lo, avg)


def _rmsnorm_wide(x, gain, avg):
    n_slab = x.shape[1] // LANES
    sq = x * x
    part = sq[:, 0:LANES]
    for i in range(1, n_slab):
        part = part + sq[:, LANES * i:LANES * (i + 1)]
    scale = lax.rsqrt(_lane_mean(part, avg) * (1.0 / n_slab) + NORM_EPS)
    return jnp.concatenate([x[:, LANES * i:LANES * (i + 1)] * scale for i in range(n_slab)], axis=1) * gain


def _mixers_kernel(h_ref, ng_ref, wnat_ref, wtm_ref, pt_ref, p_ref, wlr_ref, blr_ref, gng_ref, rng_ref, rnb_ref,
                   cos_ref, sin_ref, tri_ref, bones_ref, cmask_ref, avg_ref, rmask_ref, rkw_ref, rqw_ref, rdecq_ref, rquad_ref,
                   convw_ref, convb_ref, wlru_ref, ba_ref, bx_ref, lam_ref,
                   wb_ref, wc_ref, s5lam_ref, s5d_ref, gluw_ref, glub_ref,
                   ocat_ref,
                   ub_ref, ut_ref, proj_ref, projt_ref, gst_ref, rst_ref, lxbuf_ref, la_ref, lg_ref, lcar_ref,
                   xs_ref, hs_ref, scar_ref, otm_ref):
    j = pl.program_id(1)

    @pl.when(j == 0)
    def _():
        gst_ref[...] = jnp.zeros_like(gst_ref)
        rst_ref[...] = jnp.zeros_like(rst_ref)
        lxbuf_ref[0:CONV_HEAD, :] = jnp.zeros((CONV_HEAD, WIDTH), F32)
        lcar_ref[...] = jnp.zeros_like(lcar_ref)
        scar_ref[...] = jnp.zeros_like(scar_ref)

    @pl.when(j != 0)
    def _():
        lxbuf_ref[0:CONV_HEAD, :] = lxbuf_ref[ROWS:ROWS + CONV_HEAD, :]

    x = h_ref[...].reshape(ROWS, D_MODEL)
    ub_ref[...] = _rmsnorm_wide(x, ng_ref[...], avg_ref[0:LANES, 0:LANES]).astype(BF16)
    ut_ref[...] = _dot(pt_ref[...], ub_ref[...]).astype(BF16)

    for c0 in range(0, C_LR, PROJ_BLOCK):
        proj_ref[:, c0:c0 + PROJ_BLOCK] = _dot(ub_ref[...], wnat_ref[:, c0:c0 + PROJ_BLOCK])
    proj_ref[:, C_LR:N_NAT] = _dot(ub_ref[...], wnat_ref[:, C_LR:N_NAT])
    for c0 in range(0, N_TM, PROJ_BLOCK):
        projt_ref[:, c0:c0 + PROJ_BLOCK] = _dot(ut_ref[...], wtm_ref[:, c0:c0 + PROJ_BLOCK])

    lane = lax.broadcasted_iota(jnp.int32, (1, LANES), 1)
    head_masks = (lane < DK, lane >= DK)
    cmask = cmask_ref[...] != 0.0
    avg = avg_ref[...]
    quad_t = ((lax.broadcasted_iota(jnp.int32, (2 * DV, LANES), 0) < DV)
              == (lax.broadcasted_iota(jnp.int32, (2 * DV, LANES), 1) < DK))
    lane_k = lax.broadcasted_iota(jnp.int32, (1, KEY), 1)
    first_half = (lane_k % DK) < (DK // 2)
    cos_t = jnp.concatenate([cos_ref[...]] * SUB_CHUNKS, axis=0)
    cos_t = jnp.concatenate([cos_t] * (KEY // LANES), axis=1)
    sin_t = jnp.concatenate([sin_ref[...]] * SUB_CHUNKS, axis=0)
    sin_t = jnp.concatenate([sin_t] * (KEY // LANES), axis=1)

    def rope(t):
        swapped = jnp.where(first_half, pltpu.roll(t, KEY - DK // 2, 1), pltpu.roll(t, DK // 2, 1))
        return t * cos_t + swapped * sin_t

    for st in range(N_SUB):
        rows = slice(st * SUB, (st + 1) * SUB)

        lr = proj_ref[rows, C_LR:C_LR + LANES].astype(BF16)
        z = _dot(lr, wlr_ref[...]) + blr_ref[...]
        log_a = jax.nn.log_sigmoid(z) * (1.0 / GLA_GATE_TEMP)
        la_hl = jnp.concatenate(_split_bf16(log_a), axis=1)
        cs = _dot(tri_ref[...], la_hl)
        cum = cs[:, :KEY] + cs[:, KEY:]
        ts = _dot(bones_ref[...], la_hl)
        tot = ts[:, :KEY] + ts[:, KEY:]
        gq = proj_ref[rows, C_GQ:C_GQ + KEY]
        gk = proj_ref[rows, C_GK:C_GK + KEY]
        q_dec = gq * (DK ** -0.5) * jnp.exp(cum)
        k_dec = gk * jnp.exp(-cum)
        k_tail = gk * jnp.exp(tot - cum)
        e_dec = jnp.exp(tot)

        for p in range(PAIRS):
            pl0 = LANES * p
            pc = slice(2 * DV * p, 2 * DV * (p + 1))
            qd = q_dec[:, pl0:pl0 + LANES]
            qdb = qd.astype(BF16)
            kd = k_dec[:, pl0:pl0 + LANES].astype(BF16)
            kt = k_tail[:, pl0:pl0 + LANES].astype(BF16)
            vp = proj_ref[rows, C_GV + 2 * DV * p:C_GV + 2 * DV * (p + 1)].astype(BF16)
            o_heads = []
            for hh in range(2):
                qm = jnp.where(head_masks[hh], qd, 0.0).astype(BF16)
                sc = jnp.where(cmask, _dot_nt(qm, kd), 0.0)
                o_heads.append(_dot(sc.astype(BF16), vp[:, DV * hh:DV * (hh + 1)]))
            o = jnp.concatenate(o_heads, axis=1)
            outs = []
            for c in range(SUB_CHUNKS):
                cr = slice(CHUNK * c, CHUNK * (c + 1))
                sidx = (st * PAIRS + p) * SUB_CHUNKS + c
                state = gst_ref[sidx]
                outs.append(o[cr] + _dot_nt(qdb[cr], state.astype(BF16)))
                kv_t = _dot_tn(vp[cr], kt[cr])
                dec = e_dec[CHUNK * c:CHUNK * c + 1, pl0:pl0 + LANES]
                gst_ref[sidx] = jnp.where(quad_t, state * dec + kv_t, 0.0)
            oh = jnp.concatenate(outs, axis=0)
            oh = oh * lax.rsqrt(_lane_mean(oh * oh, avg) + NORM_EPS)
            oh = oh * gng_ref[:, pc]
            gate = proj_ref[rows, C_GG + 2 * DV * p:C_GG + 2 * DV * (p + 1)]
            ocat_ref[SUB_CHUNKS * st:SUB_CHUNKS * (st + 1), :, pc] = (
                (oh * _silu(gate)).astype(BF16).reshape(SUB_CHUNKS, CHUNK, 2 * DV))

        qr = rope(proj_ref[rows, C_RQ:C_RQ + KEY])
        kr = rope(proj_ref[rows, C_RK:C_RK + KEY]) * (DK ** -0.5)
        q_w = qr * rqw_ref[...]
        k_w = kr * rkw_ref[...]
        for p in range(PAIRS):
            pl0 = LANES * p
            pc = slice(2 * DV * p, 2 * DV * (p + 1))
            qp = qr[:, pl0:pl0 + LANES]
            qwb = q_w[:, pl0:pl0 + LANES].astype(BF16)
            kd = kr[:, pl0:pl0 + LANES].astype(BF16)
            kt = k_w[:, pl0:pl0 + LANES].astype(BF16)
            vp = proj_ref[rows, C_RV + 2 * DV * p:C_RV + 2 * DV * (p + 1)].astype(BF16)
            o_heads = []
            for hh in range(2):
                qm = jnp.where(head_masks[hh], qp, 0.0).astype(BF16)
                sc = _dot_nt(qm, kd) * rmask_ref[2 * p + hh]
                o_heads.append(_dot(sc.astype(BF16), vp[:, DV * hh:DV * (hh + 1)]))
            o = jnp.concatenate(o_heads, axis=1)
            outs = []
            for c in range(SUB_CHUNKS):
                cr = slice(CHUNK * c, CHUNK * (c + 1))
                sidx = (st * PAIRS + p) * SUB_CHUNKS + c
                state = rst_ref[sidx]
                outs.append(o[cr] + _dot(qwb[cr], state.astype(BF16)))
                kv = _dot_tn(kt[cr], vp[cr])
                rst_ref[sidx] = state * rdecq_ref[p] + kv * rquad_ref[...]
            oh = jnp.concatenate(outs, axis=0)
            cen = oh - _lane_mean(oh, avg)
            oh = cen * lax.rsqrt(_lane_mean(cen * cen, avg) + NORM_EPS)
            oh = oh * rng_ref[:, pc] + rnb_ref[:, pc]
            gate = proj_ref[rows, C_RG + 2 * DV * p:C_RG + 2 * DV * (p + 1)]
            ocat_ref[SUB_CHUNKS * st:SUB_CHUNKS * (st + 1), :, WIDTH + 2 * DV * p:WIDTH + 2 * DV * (p + 1)] = (
                (oh * _silu(gate)).astype(BF16).reshape(SUB_CHUNKS, CHUNK, 2 * DV))

    lxbuf_ref[CONV_HEAD:CONV_HEAD + ROWS, :] = projt_ref[:, T_LX:T_LX + WIDTH]
    for s in range(N_SLABS):
        ls = slice(LANES * s, LANES * (s + 1))
        xc = convb_ref[:, ls] + convw_ref[0:1, ls] * lxbuf_ref[0:ROWS, ls]
        for tap in range(1, CONV_WIDTH):
            xc = xc + convw_ref[tap:tap + 1, ls] * lxbuf_ref[GROUP * tap:GROUP * tap + ROWS, ls]
        ri = _dot(xc.astype(BF16), wlru_ref[s])
        r = _sigmoid(ri[:, :LANES] + ba_ref[:, ls])
        i_gate = _sigmoid(ri[:, LANES:] + bx_ref[:, ls])
        log_a = -LRU_C * r * jax.nn.softplus(-lam_ref[:, ls])
        a = jnp.exp(log_a)
        la_ref[:, ls] = a
        lg_ref[:, ls] = xc * i_gate * jnp.sqrt(-jnp.tanh(log_a) * (a * a + 1.0))

    def lru_body(t, hprev):
        b0 = pl.multiple_of(t * GROUP, GROUP)
        hnew = la_ref[pl.ds(b0, GROUP), :] * hprev + lg_ref[pl.ds(b0, GROUP), :]
        lg_ref[pl.ds(b0, GROUP), :] = hnew
        return hnew

    lcar_ref[...] = lax.fori_loop(0, CHUNK, lru_body, lcar_ref[...], unroll=True)
    otm_ref[:, 0:WIDTH] = (lg_ref[...] * _silu(projt_ref[:, T_LG:T_LG + WIDTH])).astype(BF16)

    half = S5_SLAB_STATE
    for s in range(N_SLABS):
        ls = slice(LANES * s, LANES * (s + 1))
        us = projt_ref[:, T_SU + LANES * s:T_SU + LANES * (s + 1)]
        xs_ref[...] = _dot(us.astype(BF16), wb_ref[s])
        lam_r = s5lam_ref[s, 0]
        lam_i = s5lam_ref[s, 1]

        def s5_body(t2, carry):
            hr, hi = carry
            b0 = pl.multiple_of(t2 * 2 * GROUP, 2 * GROUP)
            out_r, out_i = [], []
            for k in range(2):
                xr = xs_ref[pl.ds(b0 + GROUP * k, GROUP), 0:half]
                xi = xs_ref[pl.ds(b0 + GROUP * k, GROUP), half:2 * half]
                hr, hi = lam_r * hr - lam_i * hi + xr, lam_r * hi + lam_i * hr + xi
                out_r.append(hr)
                out_i.append(hi)
            hs_ref[pl.ds(b0, 2 * GROUP), 0:half] = jnp.concatenate(out_r, axis=0).astype(BF16)
            hs_ref[pl.ds(b0, 2 * GROUP), half:2 * half] = jnp.concatenate(out_i, axis=0).astype(BF16)
            return hr, hi

        cr, ci = lax.fori_loop(0, CHUNK // 2, s5_body, (scar_ref[s, 0], scar_ref[s, 1]), unroll=True)
        scar_ref[s, 0] = cr
        scar_ref[s, 1] = ci
        y = _dot(hs_ref[...], wc_ref[s]) + s5d_ref[:, ls] * us
        lg_ref[:, ls] = _gelu_tanh(y)
    y5 = lg_ref[...]
    glu = _sigmoid(_dot(y5.astype(BF16), gluw_ref[...]) + glub_ref[...])
    otm_ref[:, WIDTH:2 * WIDTH] = (y5 * glu * _silu(projt_ref[:, T_SG:T_SG + WIDTH])).astype(BF16)

    onat = _dot(p_ref[...], otm_ref[...]).astype(BF16)
    ocat_ref[:, :, 2 * WIDTH:4 * WIDTH] = onat.reshape(GROUP, CHUNK, 2 * WIDTH)


def _merge_kernel(h_ref, ocat_ref, ng_ref, wg_ref, wbr_ref, wout_ref, fg_ref, out_ref, ub_ref, m_ref,
                  *, final_norm):
    x = h_ref[...]
    ub_ref[...] = _rmsnorm_rows(x, ng_ref[...]).astype(BF16)
    for nb in range(D_MODEL // MERGE_COLS):
        cols = slice(MERGE_COLS * nb, MERGE_COLS * (nb + 1))
        merged = None
        for bi in range(N_BRANCHES):
            gate = _sigmoid(_dot(ub_ref[...], wg_ref[bi, :, cols]))
            branch = _dot(ocat_ref[:, WIDTH * bi:WIDTH * (bi + 1)], wbr_ref[bi, :, cols])
            term = gate * branch
            merged = term if merged is None else merged + term
        m_ref[:, cols] = merged.astype(BF16)
    out = x + _dot(m_ref[...], wout_ref[...])
    if final_norm:
        out = _rmsnorm_rows(out, fg_ref[...])
    out_ref[...] = out


def _const_spec(shape):
    nd = len(shape)
    return pl.BlockSpec(shape, lambda b, j, _nd=nd: (0,) * _nd, pipeline_mode=pl.Buffered(1))


def _block_diag(blocks):
    n, r, c = blocks.shape
    eye = jnp.eye(n, dtype=blocks.dtype)
    return (blocks[:, :, None, :] * eye[:, None, :, None]).reshape(n * r, n * c)


def _tile_constants():
    pos = jnp.arange(SUB)
    same_chunk = (pos[:, None] // CHUNK) == (pos[None, :] // CHUNK)
    causal = same_chunk & (pos[:, None] >= pos[None, :])
    tri = causal.astype(BF16)
    bones = same_chunk.astype(BF16)
    cmask = causal.astype(F32)
    avg = _block_diag(jnp.full((2, LANES, LANES), 1.0 / LANES, F32)).astype(BF16)
    log_gamma = jnp.log1p(-jnp.exp2(-5.0 - jnp.arange(HEADS, dtype=F32)))
    rel = (pos[:, None] - pos[None, :]).astype(F32)
    decay = jnp.exp(jnp.maximum(rel, 0.0)[None] * log_gamma[:, None, None])
    rmask = jnp.where(causal[None], decay, 0.0).astype(F32)
    cpos = (pos % CHUNK).astype(F32)
    k_w = jnp.exp((CHUNK - 1.0 - cpos)[:, None] * log_gamma[None, :])
    q_w = jnp.exp((cpos + 1.0)[:, None] * log_gamma[None, :])
    rkw = jnp.repeat(k_w, DK, axis=1)
    rqw = jnp.repeat(q_w, DK, axis=1)
    rquad = _block_diag(jnp.ones((2, DK, DV), F32))
    rdecq = jnp.exp(CHUNK * log_gamma).reshape(PAIRS, 2)
    rdecq = jnp.stack([_block_diag(rdecq[p][:, None, None] * jnp.ones((2, DK, DV), F32)) for p in range(PAIRS)])
    r_tm = jnp.arange(ROWS)
    src = (r_tm % GROUP) * CHUNK + r_tm // GROUP
    pt = (src[:, None] == jnp.arange(ROWS)[None, :]).astype(BF16)
    return dict(tri=tri, bones=bones, cmask=cmask, avg=avg, rmask=rmask, rkw=rkw, rqw=rqw, rdecq=rdecq, rquad=rquad,
                pt=pt, p=pt.T)


def _rope_tables(seq):
    half = DK // 2
    inv = ROPE_BASE ** (-jnp.arange(half, dtype=F32) / half)
    ang = jnp.arange(seq, dtype=F32)[:, None] * inv[None, :]
    cos = jnp.cos(ang)
    sin = jnp.sin(ang)
    cos_t = jnp.concatenate([cos, cos] * (LANES // DK), axis=1)
    sin_t = jnp.concatenate([-sin, sin] * (LANES // DK), axis=1)
    return cos_t, sin_t


def _s5_params(a_re, a_im, b_re, b_im, c_re, c_im, log_dt):
    step = jnp.exp(log_dt)[:, None]
    mag = jnp.exp(step * a_re)
    ab_re = mag * jnp.cos(step * a_im)
    ab_im = mag * jnp.sin(step * a_im)
    den = a_re * a_re + a_im * a_im
    f_re = ((ab_re - 1.0) * a_re + ab_im * a_im) / den
    f_im = (ab_im * a_re - (ab_re - 1.0) * a_im) / den
    bb_re = f_re[..., None] * b_re - f_im[..., None] * b_im
    bb_im = f_re[..., None] * b_im + f_im[..., None] * b_re
    gps = LANES // S5_GROUP
    wb, wc = [], []
    for s in range(N_SLABS):
        g = slice(gps * s, gps * (s + 1))
        wb.append(jnp.concatenate([_block_diag(jnp.swapaxes(bb_re[g], 1, 2)),
                                   _block_diag(jnp.swapaxes(bb_im[g], 1, 2))], axis=1))
        wc.append(jnp.concatenate([_block_diag(jnp.swapaxes(c_re[g], 1, 2)),
                                   _block_diag(jnp.swapaxes(-c_im[g], 1, 2))], axis=0))
    wb = jnp.stack(wb).astype(BF16)
    wc = jnp.stack(wc).astype(BF16)
    lam = jnp.stack([ab_re.reshape(N_SLABS, S5_SLAB_STATE), ab_im.reshape(N_SLABS, S5_SLAB_STATE)], axis=1)
    lam = jnp.broadcast_to(lam[:, :, None, :], (N_SLABS, 2, GROUP, S5_SLAB_STATE)).astype(F32)
    return wb, wc, lam


def _lru_weights(w_a, w_x):
    per = LANES // LRU_BLOCK_DIM
    out = []
    for s in range(N_SLABS):
        blk = slice(per * s, per * (s + 1))
        out.append(jnp.concatenate([_block_diag(w_a[blk]), _block_diag(w_x[blk])], axis=1))
    return jnp.stack(out).astype(BF16)


def _repack_w_in(w):
    lr0 = 2 * KEY + WIDTH
    tm0 = lr0 + GLA_RANK + 2 * WIDTH + 2 * KEY + WIDTH
    pad = jnp.zeros((D_MODEL, LANES - GLA_RANK), w.dtype)
    w_nat = jnp.concatenate([w[:, :lr0], w[:, lr0 + GLA_RANK:tm0], w[:, lr0:lr0 + GLA_RANK], pad], axis=1)
    return w_nat.astype(BF16), w[:, tm0:].astype(BF16)


def _mixers_call(h3, p, consts, cos_t, sin_t):
    batch, seq, _ = h3.shape
    time_spec = pl.BlockSpec((CHUNK, LANES), lambda g, j: (j, 0))
    c = consts
    operands = [
        (h3, pl.BlockSpec((GROUP, CHUNK, D_MODEL), lambda g, j: (g, j, 0))),
        (p['norm_gain'], None), (p['w_nat'], None), (p['w_tm'], None), (c['pt'], None), (c['p'], None),
        (p['w_lr'], None), (p['b_lr'], None),
        (p['gla_gain'], None), (p['ret_gain'], None), (p['ret_bias'], None),
        (cos_t, time_spec), (sin_t, time_spec),
        (c['tri'], None), (c['bones'], None), (c['cmask'], None), (c['avg'], None), (c['rmask'], None),
        (c['rkw'], None), (c['rqw'], None), (c['rdecq'], None), (c['rquad'], None),
        (p['conv_w'], None), (p['conv_b'], None), (p['w_lru'], None), (p['b_a'], None), (p['b_x'], None),
        (p['lam'], None),
        (p['s5_wb'], None), (p['s5_wc'], None), (p['s5_lam'], None), (p['s5_d'], None),
        (p['glu_w'], None), (p['glu_b'], None),
    ]
    args = [a for a, _ in operands]
    specs = [s if s is not None else _const_spec(a.shape) for a, s in operands]
    scratch = [
        pltpu.VMEM((ROWS, D_MODEL), BF16),
        pltpu.VMEM((ROWS, D_MODEL), BF16),
        pltpu.VMEM((ROWS, N_NAT), F32),
        pltpu.VMEM((ROWS, N_TM), F32),
        pltpu.VMEM((GROUP * PAIRS, 2 * DV, LANES), F32),
        pltpu.VMEM((GROUP * PAIRS, LANES, 2 * DV), F32),
        pltpu.VMEM((ROWS + CONV_HEAD, WIDTH), F32),
        pltpu.VMEM((ROWS, WIDTH), F32),
        pltpu.VMEM((ROWS, WIDTH), F32),
        pltpu.VMEM((GROUP, WIDTH), F32),
        pltpu.VMEM((ROWS, 2 * S5_SLAB_STATE), F32),
        pltpu.VMEM((ROWS, 2 * S5_SLAB_STATE), BF16),
        pltpu.VMEM((N_SLABS, 2, GROUP, S5_SLAB_STATE), F32),
        pltpu.VMEM((ROWS, 2 * WIDTH), BF16),
    ]
    return pl.pallas_call(
        _mixers_kernel,
        grid=(batch // GROUP, seq // CHUNK),
        in_specs=specs,
        out_specs=pl.BlockSpec((GROUP, CHUNK, N_BRANCHES * WIDTH), lambda g, j: (g, j, 0)),
        out_shape=jax.ShapeDtypeStruct((batch, seq, N_BRANCHES * WIDTH), BF16),
        scratch_shapes=scratch,
        compiler_params=pltpu.CompilerParams(dimension_semantics=("arbitrary", "arbitrary"),
                                             vmem_limit_bytes=VMEM_LIMIT_BYTES),
        name="mixers",
    )(*args)


def _merge_call(h2, ocat, p, final_gain, final_norm):
    rows = h2.shape[0]
    tile = min(MERGE_TILE, rows)
    row_spec = lambda cols: pl.BlockSpec((tile, cols), lambda i, j: (i, 0))
    args = [h2, ocat, p['norm_gain'], p['w_gate'], p['w_branch'], p['w_out'], final_gain]
    specs = [row_spec(D_MODEL), row_spec(N_BRANCHES * WIDTH)] + [_const_spec(a.shape) for a in args[2:]]
    return pl.pallas_call(
        functools.partial(_merge_kernel, final_norm=final_norm),
        grid=(rows // tile, 1),
        in_specs=specs,
        out_specs=row_spec(D_MODEL),
        out_shape=jax.ShapeDtypeStruct((rows, D_MODEL), F32),
        scratch_shapes=[pltpu.VMEM((tile, D_MODEL), BF16), pltpu.VMEM((tile, D_MODEL), BF16)],
        compiler_params=pltpu.CompilerParams(dimension_semantics=("arbitrary", "arbitrary"),
                                             vmem_limit_bytes=VMEM_LIMIT_BYTES),
        name="merge",
    )(*args)


@jax.jit
def _forward(x, norm_gain, w_in, gla_w_lr, gla_b_lr, gla_norm_gain, ret_norm_gain, ret_norm_bias,
             lru_conv_w, lru_conv_b, lru_w_a, lru_b_a, lru_w_x, lru_b_x, lru_lambda,
             s5_a_re, s5_a_im, s5_b_re, s5_b_im, s5_c_re, s5_c_im, s5_d, s5_log_dt, s5_glu_w, s5_glu_b,
             w_merge_gate, w_branch, w_out, final_norm_gain):
    batch, seq, _ = x.shape
    depth = w_in.shape[0]
    assert batch % GROUP == 0 and seq % CHUNK == 0 and (batch * seq) % min(MERGE_TILE, batch * seq) == 0
    consts = _tile_constants()
    cos_t, sin_t = _rope_tables(seq)
    row = lambda v: v.astype(F32)[None, :]
    h3 = x.astype(F32)
    for l in range(depth):
        s5_wb, s5_wc, s5_lam = _s5_params(s5_a_re[l], s5_a_im[l], s5_b_re[l], s5_b_im[l],
                                          s5_c_re[l], s5_c_im[l], s5_log_dt[l])
        w_nat, w_tm = _repack_w_in(w_in[l])
        p = {
            'norm_gain': row(norm_gain[l]),
            'w_nat': w_nat, 'w_tm': w_tm,
            'w_lr': jnp.concatenate([gla_w_lr[l], jnp.zeros((LANES - GLA_RANK, KEY), F32)], axis=0).astype(BF16),
            'b_lr': row(gla_b_lr[l]),
            'gla_gain': row(gla_norm_gain[l]),
            'ret_gain': row(ret_norm_gain[l]),
            'ret_bias': row(ret_norm_bias[l]),
            'conv_w': lru_conv_w[l].astype(F32),
            'conv_b': row(lru_conv_b[l]),
            'w_lru': _lru_weights(lru_w_a[l], lru_w_x[l]),
            'b_a': row(lru_b_a[l]),
            'b_x': row(lru_b_x[l]),
            'lam': row(lru_lambda[l]),
            's5_wb': s5_wb, 's5_wc': s5_wc, 's5_lam': s5_lam,
            's5_d': row(s5_d[l]),
            'glu_w': s5_glu_w[l].astype(BF16),
            'glu_b': row(s5_glu_b[l]),
            'w_gate': w_merge_gate[l].astype(BF16),
            'w_branch': w_branch[l].astype(BF16),
            'w_out': w_out[l].astype(BF16),
        }
        ocat = _mixers_call(h3, p, consts, cos_t, sin_t)
        h2 = _merge_call(h3.reshape(batch * seq, D_MODEL), ocat.reshape(batch * seq, N_BRANCHES * WIDTH), p,
                         row(final_norm_gain), final_norm=(l == depth - 1))
        h3 = h2.reshape(batch, seq, D_MODEL)
    return h3.astype(x.dtype)


def kernel(x, norm_gain, w_in, gla_w_lr, gla_b_lr, gla_norm_gain, ret_norm_gain, ret_norm_bias, lru_conv_w,
           lru_conv_b, lru_w_a, lru_b_a, lru_w_x, lru_b_x, lru_lambda, s5_a_re, s5_a_im, s5_b_re, s5_b_im,
           s5_c_re, s5_c_im, s5_d, s5_log_dt, s5_glu_w, s5_glu_b, w_merge_gate, w_branch, w_out,
           final_norm_gain):
    return _forward(x, norm_gain, w_in, gla_w_lr, gla_b_lr, gla_norm_gain, ret_norm_gain, ret_norm_bias,
                    lru_conv_w, lru_conv_b, lru_w_a, lru_b_a, lru_w_x, lru_b_x, lru_lambda,
                    s5_a_re, s5_a_im, s5_b_re, s5_b_im, s5_c_re, s5_c_im, s5_d, s5_log_dt, s5_glu_w, s5_glu_b,
                    w_merge_gate, w_branch, w_out, final_norm_gain)
```

```python
import functools
import math

import jax
import jax.numpy as jnp
from jax import lax
from jax.experimental import pallas as pl
from jax.experimental.pallas import tpu as pltpu

F32 = jnp.float32
BF16 = jnp.bfloat16

D_MODEL = 1024
N_BRANCHES = 4
WIDTH = 512
NORM_EPS = 1e-6
CHUNK = 64
HEADS = 4
DK = 64
DV = WIDTH // HEADS
KEY = HEADS * DK
GLA_RANK = 16
GLA_GATE_TEMP = 16.0
ROPE_BASE = 10000.0
LRU_BLOCKS = 8
LRU_BLOCK_DIM = WIDTH // LRU_BLOCKS
CONV_WIDTH = 4
LRU_C = 8.0
S5_GROUP = 16
S5_GROUPS = WIDTH // S5_GROUP
S5_STATE = 64

LANES = 128
SUBLANES = 8
VMEM_LIMIT_BYTES = 60 * 1024 * 1024

GROUP = SUBLANES
ROWS = GROUP * CHUNK
SUB = 256
SUB_CHUNKS = SUB // CHUNK
N_SUB = ROWS // SUB
PAIRS = HEADS // 2

C_GQ, C_GK, C_GV, C_GG = 0, 256, 512, 1024
C_RQ, C_RK, C_RV, C_RG = 1536, 1792, 2048, 2560
C_LR = 3072
N_NAT = C_LR + LANES
T_LX, T_LG, T_SU, T_SG = 0, 512, 1024, 1536
N_TM = 2048
PROJ_BLOCK = 512
NAT_PIECE = 256
N_SLABS = WIDTH // LANES
S5_SLAB_STATE = (LANES // S5_GROUP) * S5_STATE
CONV_HEAD = (CONV_WIDTH - 1) * GROUP

MERGE_TILE = 512
MERGE_COLS = 256


def _sigmoid(x):
    return jax.nn.sigmoid(x)


def _silu(x):
    return x * jax.nn.sigmoid(x)


def _gelu_tanh(x):
    c = math.sqrt(2.0 / math.pi)
    return 0.5 * x * (1.0 + jnp.tanh(c * (x + 0.044715 * (x * x * x))))


def _dot(a, b):
    return jnp.dot(a, b, preferred_element_type=F32)


def _dot_nt(a, b):
    return lax.dot_general(a, b, (((1,), (1,)), ((), ())), preferred_element_type=F32)


def _dot_tn(a, b):
    return lax.dot_general(a, b, (((0,), (0,)), ((), ())), preferred_element_type=F32)


def _rmsnorm_rows(x, gain):
    return x * lax.rsqrt(jnp.mean(x * x, axis=-1, keepdims=True) + NORM_EPS) * gain


def _split_bf16(x):
    hi = x.astype(BF16)
    lo = (x - hi.astype(F32)).astype(BF16)
    return hi, lo


def _lane_mean(x, avg):
    return _dot(x.astype(BF16), avg)


def _rmsnorm_wide(x, gain, avg):
    n_slab = x.shape[1] // LANES
    sq = x * x
    part = sq[:, 0:LANES]
    for i in range(1, n_slab):
        part = part + sq[:, LANES * i:LANES * (i + 1)]
    scale = lax.rsqrt(_lane_mean(part, avg) * (1.0 / n_slab) + NORM_EPS)
    return jnp.concatenate([x[:, LANES * i:LANES * (i + 1)] * scale for i in range(n_slab)], axis=1) * gain


def _mixers_kernel(hfirst_ref, hnext_ref, ng_ref, wnat_ref, wtm_ref, wlr_ref, blr_ref, gng_ref, rng_ref, rnb_ref,
                   cos_ref, sin_ref, tri_ref, cmask_ref, avg_ref, rmask_ref, rkw_ref, rqw_ref, rdecq_ref, rquad_ref,
                   convw_ref, convb_ref, wlru_ref, ba_ref, bx_ref, lam_ref,
                   wb_ref, wc_ref, s5lam_ref, s5d_ref, gluw_ref, glub_ref,
                   ocat_ref,
                   ub_ref, ut_ref, proj_ref, projt_ref, gst_ref, rst_ref, lxbuf_ref, la_ref, lg_ref, lcar_ref,
                   xs_ref, hs_ref, scar_ref, perm_ref, stage_ref):
    j = pl.program_id(1)

    def normalise_tile(src_ref):
        x = src_ref[...].reshape(ROWS, D_MODEL)
        ub_ref[...] = _rmsnorm_wide(x, ng_ref[...], avg_ref[0:LANES, 0:LANES]).astype(BF16)
        for i in range(D_MODEL // LANES):
            for b in range(GROUP):
                stage_ref[i, pl.ds(b, CHUNK, stride=GROUP), :] = (
                    ub_ref[CHUNK * b:CHUNK * (b + 1), LANES * i:LANES * (i + 1)].astype(F32))
            ut_ref[:, LANES * i:LANES * (i + 1)] = stage_ref[i].astype(BF16)

    @pl.when(j == 0)
    def _():
        normalise_tile(hfirst_ref)
        gst_ref[...] = jnp.zeros_like(gst_ref)
        rst_ref[...] = jnp.zeros_like(rst_ref)
        lxbuf_ref[0:CONV_HEAD, :] = jnp.zeros((CONV_HEAD, WIDTH), F32)
        lcar_ref[...] = jnp.zeros_like(lcar_ref)
        scar_ref[...] = jnp.zeros_like(scar_ref)

    @pl.when(j != 0)
    def _():
        lxbuf_ref[0:CONV_HEAD, :] = lxbuf_ref[ROWS:ROWS + CONV_HEAD, :]

    for c0 in range(0, N_TM, PROJ_BLOCK):
        projt_ref[:, c0:c0 + PROJ_BLOCK] = _dot(ut_ref[...], wtm_ref[:, c0:c0 + PROJ_BLOCK])

    nat_blocks = [(c0, c0 + NAT_PIECE) for c0 in range(0, C_LR, NAT_PIECE)] + [(C_LR, N_NAT)]

    def project_next_block():
        if nat_blocks:
            c0, c1 = nat_blocks.pop(0)
            proj_ref[:, c0:c1] = _dot(ub_ref[...], wnat_ref[:, c0:c1])

    lxbuf_ref[CONV_HEAD:CONV_HEAD + ROWS, :] = projt_ref[:, T_LX:T_LX + WIDTH]
    for s in range(N_SLABS):
        ls = slice(LANES * s, LANES * (s + 1))
        xc = convb_ref[:, ls] + convw_ref[0:1, ls] * lxbuf_ref[0:ROWS, ls]
        for tap in range(1, CONV_WIDTH):
            xc = xc + convw_ref[tap:tap + 1, ls] * lxbuf_ref[GROUP * tap:GROUP * tap + ROWS, ls]
        ri = _dot(xc.astype(BF16), wlru_ref[s])
        r = _sigmoid(ri[:, :LANES] + ba_ref[:, ls])
        i_gate = _sigmoid(ri[:, LANES:] + bx_ref[:, ls])
        log_a = -LRU_C * r * jax.nn.softplus(-lam_ref[:, ls])
        a = jnp.exp(log_a)
        la_ref[:, ls] = a
        lg_ref[:, ls] = xc * i_gate * jnp.sqrt(-jnp.tanh(log_a) * (a * a + 1.0))
        project_next_block()

    def lru_body(t, hprev):
        b0 = pl.multiple_of(t * GROUP, GROUP)
        hnew = la_ref[pl.ds(b0, GROUP), :] * hprev + lg_ref[pl.ds(b0, GROUP), :]
        lg_ref[pl.ds(b0, GROUP), :] = hnew
        return hnew

    lcar_ref[...] = lax.fori_loop(0, CHUNK, lru_body, lcar_ref[...], unroll=True)
    for s in range(N_SLABS):
        ls = slice(LANES * s, LANES * (s + 1))
        perm_ref[s] = lg_ref[:, ls] * _silu(projt_ref[:, T_LG + LANES * s:T_LG + LANES * (s + 1)])
    project_next_block()

    half = S5_SLAB_STATE
    for s in range(N_SLABS):
        ls = slice(LANES * s, LANES * (s + 1))
        buf = s % 2
        us = projt_ref[:, T_SU + LANES * s:T_SU + LANES * (s + 1)]
        xs_ref[buf] = _dot(us.astype(BF16), wb_ref[s])
        project_next_block()
        lam_r = s5lam_ref[s, 0]
        lam_i = s5lam_ref[s, 1]

        def s5_body(t2, carry, buf=buf, lam_r=lam_r, lam_i=lam_i):
            hr, hi = carry
            b0 = pl.multiple_of(t2 * 2 * GROUP, 2 * GROUP)
            out_r, out_i = [], []
            for k in range(2):
                xr = xs_ref[buf, pl.ds(b0 + GROUP * k, GROUP), 0:half]
                xi = xs_ref[buf, pl.ds(b0 + GROUP * k, GROUP), half:2 * half]
                hr, hi = lam_r * hr - lam_i * hi + xr, lam_r * hi + lam_i * hr + xi
                out_r.append(hr)
                out_i.append(hi)
            hs_ref[buf, pl.ds(b0, 2 * GROUP), 0:half] = jnp.concatenate(out_r, axis=0).astype(BF16)
            hs_ref[buf, pl.ds(b0, 2 * GROUP), half:2 * half] = jnp.concatenate(out_i, axis=0).astype(BF16)
            return hr, hi

        cr, ci = lax.fori_loop(0, CHUNK // 2, s5_body, (scar_ref[s, 0], scar_ref[s, 1]), unroll=True)
        scar_ref[s, 0] = cr
        scar_ref[s, 1] = ci
        y = _dot(hs_ref[buf], wc_ref[s]) + s5d_ref[:, ls] * us
        lg_ref[:, ls] = _gelu_tanh(y)
        project_next_block()
    y5 = lg_ref[...]
    glu = _sigmoid(_dot(y5.astype(BF16), gluw_ref[...]) + glub_ref[...])
    o_s5 = y5 * glu * _silu(projt_ref[:, T_SG:T_SG + WIDTH])
    for s in range(N_SLABS):
        perm_ref[N_SLABS + s] = o_s5[:, LANES * s:LANES * (s + 1)]

    for i in range(2 * N_SLABS):
        for b in range(GROUP):
            ocat_ref[b, :, 2 * WIDTH + LANES * i:2 * WIDTH + LANES * (i + 1)] = (
                perm_ref[i, pl.ds(b, CHUNK, stride=GROUP), :].astype(BF16))
    while nat_blocks:
        project_next_block()

    normalise_tile(hnext_ref)

    lane = lax.broadcasted_iota(jnp.int32, (1, LANES), 1)
    head_masks = (lane < DK, lane >= DK)
    cmask = cmask_ref[...] != 0.0
    avg = avg_ref[...]
    quad_t = ((lax.broadcasted_iota(jnp.int32, (2 * DV, LANES), 0) < DV)
              == (lax.broadcasted_iota(jnp.int32, (2 * DV, LANES), 1) < DK))
    lane_k = lax.broadcasted_iota(jnp.int32, (1, KEY), 1)
    first_half = (lane_k % DK) < (DK // 2)
    cos_t = jnp.concatenate([cos_ref[...]] * SUB_CHUNKS, axis=0)
    cos_t = jnp.concatenate([cos_t] * (KEY // LANES), axis=1)
    sin_t = jnp.concatenate([sin_ref[...]] * SUB_CHUNKS, axis=0)
    sin_t = jnp.concatenate([sin_t] * (KEY // LANES), axis=1)

    def rope(t):
        swapped = jnp.where(first_half, pltpu.roll(t, KEY - DK // 2, 1), pltpu.roll(t, DK // 2, 1))
        return t * cos_t + swapped * sin_t

    finished = []
    for st in range(N_SUB):
        rows = slice(st * SUB, (st + 1) * SUB)
        out_rows = slice(SUB_CHUNKS * st, SUB_CHUNKS * (st + 1))

        lr = proj_ref[rows, C_LR:C_LR + LANES].astype(BF16)
        z = _dot(lr, wlr_ref[...]) + blr_ref[...]
        log_a = jax.nn.log_sigmoid(z) * (1.0 / GLA_GATE_TEMP)
        la_hl = jnp.concatenate(_split_bf16(log_a), axis=1)
        cs = _dot(tri_ref[...], la_hl)
        cum = cs[:, :KEY] + cs[:, KEY:]
        tot = jnp.concatenate(
            [jnp.broadcast_to(cum[CHUNK * (c + 1) - 1:CHUNK * (c + 1), :], (CHUNK, KEY)) for c in range(SUB_CHUNKS)],
            axis=0)
        gq = proj_ref[rows, C_GQ:C_GQ + KEY]
        gk = proj_ref[rows, C_GK:C_GK + KEY]
        q_dec = gq * (DK ** -0.5) * jnp.exp(cum)
        k_dec = gk * jnp.exp(-cum)
        k_tail = gk * jnp.exp(tot - cum)
        e_dec = jnp.exp(tot)
        qr = rope(proj_ref[rows, C_RQ:C_RQ + KEY])
        kr = rope(proj_ref[rows, C_RK:C_RK + KEY]) * (DK ** -0.5)
        q_w = qr * rqw_ref[...]
        k_w = kr * rkw_ref[...]

        units = []
        for p in range(PAIRS):
            pk = slice(LANES * p, LANES * (p + 1))
            units.append(dict(
                gla=True, p=p, q=q_dec[:, pk], q_inter=q_dec[:, pk].astype(BF16), kd=k_dec[:, pk].astype(BF16),
                kt=k_tail[:, pk].astype(BF16), dec=e_dec[:, pk],
                vp=proj_ref[rows, C_GV + 2 * DV * p:C_GV + 2 * DV * (p + 1)].astype(BF16)))
        for p in range(PAIRS):
            pk = slice(LANES * p, LANES * (p + 1))
            units.append(dict(
                gla=False, p=p, q=qr[:, pk], q_inter=q_w[:, pk].astype(BF16), kd=kr[:, pk].astype(BF16),
                kt=k_w[:, pk].astype(BF16),
                vp=proj_ref[rows, C_RV + 2 * DV * p:C_RV + 2 * DV * (p + 1)].astype(BF16)))

        for u in units:
            u['sc'] = []
            for hh in range(2):
                qm = jnp.where(head_masks[hh], u['q'], 0.0).astype(BF16)
                sc = _dot_nt(qm, u['kd'])
                sc = jnp.where(cmask, sc, 0.0) if u['gla'] else sc * rmask_ref[2 * u['p'] + hh]
                u['sc'].append(sc.astype(BF16))
        for u in units:
            u['o'] = jnp.concatenate([_dot(u['sc'][hh], u['vp'][:, DV * hh:DV * (hh + 1)]) for hh in range(2)], axis=1)
        for u in units:
            outs = []
            for c in range(SUB_CHUNKS):
                cr = slice(CHUNK * c, CHUNK * (c + 1))
                sidx = (st * PAIRS + u['p']) * SUB_CHUNKS + c
                if u['gla']:
                    state = gst_ref[sidx]
                    outs.append(u['o'][cr] + _dot_nt(u['q_inter'][cr], state.astype(BF16)))
                    kv_t = _dot_tn(u['vp'][cr], u['kt'][cr])
                    gst_ref[sidx] = jnp.where(quad_t, state * u['dec'][CHUNK * c:CHUNK * c + 1, :] + kv_t, 0.0)
                else:
                    state = rst_ref[sidx]
                    outs.append(u['o'][cr] + _dot(u['q_inter'][cr], state.astype(BF16)))
                    kv = _dot_tn(u['kt'][cr], u['vp'][cr])
                    rst_ref[sidx] = state * rdecq_ref[u['p']] + kv * rquad_ref[...]
            u['oh'] = jnp.concatenate(outs, axis=0)
            u['rows'], u['out_rows'] = rows, out_rows
        finished.append(units)

    for units in finished:
        for u in units:
            rows, out_rows = u['rows'], u['out_rows']
            pc = slice(2 * DV * u['p'], 2 * DV * (u['p'] + 1))
            oh = u['oh']
            if u['gla']:
                oh = oh * lax.rsqrt(_lane_mean(oh * oh, avg) + NORM_EPS) * gng_ref[:, pc]
                gate = proj_ref[rows, C_GG + 2 * DV * u['p']:C_GG + 2 * DV * (u['p'] + 1)]
                out_cols = pc
            else:
                cen = oh - _lane_mean(oh, avg)
                oh = cen * lax.rsqrt(_lane_mean(cen * cen, avg) + NORM_EPS)
                oh = oh * rng_ref[:, pc] + rnb_ref[:, pc]
                gate = proj_ref[rows, C_RG + 2 * DV * u['p']:C_RG + 2 * DV * (u['p'] + 1)]
                out_cols = slice(WIDTH + 2 * DV * u['p'], WIDTH + 2 * DV * (u['p'] + 1))
            ocat_ref[out_rows, :, out_cols] = (oh * _silu(gate)).astype(BF16).reshape(SUB_CHUNKS, CHUNK, 2 * DV)


def _merge_kernel(h_ref, ocat_ref, ng_ref, wg_ref, wbr_ref, wout_ref, fg_ref, out_ref, ub_ref, m_ref,
                  *, final_norm):
    x = h_ref[...]
    ub_ref[...] = _rmsnorm_rows(x, ng_ref[...]).astype(BF16)
    for nb in range(D_MODEL // MERGE_COLS):
        cols = slice(MERGE_COLS * nb, MERGE_COLS * (nb + 1))
        merged = None
        for bi in range(N_BRANCHES):
            gate = _sigmoid(_dot(ub_ref[...], wg_ref[bi, :, cols]))
            branch = _dot(ocat_ref[:, WIDTH * bi:WIDTH * (bi + 1)], wbr_ref[bi, :, cols])
            term = gate * branch
            merged = term if merged is None else merged + term
        m_ref[:, cols] = merged.astype(BF16)
    out = x + _dot(m_ref[...], wout_ref[...])
    if final_norm:
        out = _rmsnorm_rows(out, fg_ref[...])
    out_ref[...] = out


def _const_spec(shape):
    nd = len(shape)
    return pl.BlockSpec(shape, lambda b, j, _nd=nd: (0,) * _nd, pipeline_mode=pl.Buffered(1))


def _block_diag(blocks):
    n, r, c = blocks.shape
    eye = jnp.eye(n, dtype=blocks.dtype)
    return (blocks[:, :, None, :] * eye[:, None, :, None]).reshape(n * r, n * c)


def _tile_constants():
    pos = jnp.arange(SUB)
    same_chunk = (pos[:, None] // CHUNK) == (pos[None, :] // CHUNK)
    causal = same_chunk & (pos[:, None] >= pos[None, :])
    tri = causal.astype(BF16)
    cmask = causal.astype(F32)
    avg = _block_diag(jnp.full((2, LANES, LANES), 1.0 / LANES, F32)).astype(BF16)
    log_gamma = jnp.log1p(-jnp.exp2(-5.0 - jnp.arange(HEADS, dtype=F32)))
    rel = (pos[:, None] - pos[None, :]).astype(F32)
    decay = jnp.exp(jnp.maximum(rel, 0.0)[None] * log_gamma[:, None, None])
    rmask = jnp.where(causal[None], decay, 0.0).astype(F32)
    cpos = (pos % CHUNK).astype(F32)
    k_w = jnp.exp((CHUNK - 1.0 - cpos)[:, None] * log_gamma[None, :])
    q_w = jnp.exp((cpos + 1.0)[:, None] * log_gamma[None, :])
    rkw = jnp.repeat(k_w, DK, axis=1)
    rqw = jnp.repeat(q_w, DK, axis=1)
    rquad = _block_diag(jnp.ones((2, DK, DV), F32))
    rdecq = jnp.exp(CHUNK * log_gamma).reshape(PAIRS, 2)
    rdecq = jnp.stack([_block_diag(rdecq[p][:, None, None] * jnp.ones((2, DK, DV), F32)) for p in range(PAIRS)])
    return dict(tri=tri, cmask=cmask, avg=avg, rmask=rmask, rkw=rkw, rqw=rqw, rdecq=rdecq, rquad=rquad)


def _rope_tables(seq):
    half = DK // 2
    inv = ROPE_BASE ** (-jnp.arange(half, dtype=F32) / half)
    ang = jnp.arange(seq, dtype=F32)[:, None] * inv[None, :]
    cos = jnp.cos(ang)
    sin = jnp.sin(ang)
    cos_t = jnp.concatenate([cos, cos] * (LANES // DK), axis=1)
    sin_t = jnp.concatenate([-sin, sin] * (LANES // DK), axis=1)
    return cos_t, sin_t


def _s5_params(a_re, a_im, b_re, b_im, c_re, c_im, log_dt):
    step = jnp.exp(log_dt)[:, None]
    mag = jnp.exp(step * a_re)
    ab_re = mag * jnp.cos(step * a_im)
    ab_im = mag * jnp.sin(step * a_im)
    den = a_re * a_re + a_im * a_im
    f_re = ((ab_re - 1.0) * a_re + ab_im * a_im) / den
    f_im = (ab_im * a_re - (ab_re - 1.0) * a_im) / den
    bb_re = f_re[..., None] * b_re - f_im[..., None] * b_im
    bb_im = f_re[..., None] * b_im + f_im[..., None] * b_re
    gps = LANES // S5_GROUP
    wb, wc = [], []
    for s in range(N_SLABS):
        g = slice(gps * s, gps * (s + 1))
        wb.append(jnp.concatenate([_block_diag(jnp.swapaxes(bb_re[g], 1, 2)),
                                   _block_diag(jnp.swapaxes(bb_im[g], 1, 2))], axis=1))
        wc.append(jnp.concatenate([_block_diag(jnp.swapaxes(c_re[g], 1, 2)),
                                   _block_diag(jnp.swapaxes(-c_im[g], 1, 2))], axis=0))
    wb = jnp.stack(wb).astype(BF16)
    wc = jnp.stack(wc).astype(BF16)
    lam = jnp.stack([ab_re.reshape(N_SLABS, S5_SLAB_STATE), ab_im.reshape(N_SLABS, S5_SLAB_STATE)], axis=1)
    lam = jnp.broadcast_to(lam[:, :, None, :], (N_SLABS, 2, GROUP, S5_SLAB_STATE)).astype(F32)
    return wb, wc, lam


def _lru_weights(w_a, w_x):
    per = LANES // LRU_BLOCK_DIM
    out = []
    for s in range(N_SLABS):
        blk = slice(per * s, per * (s + 1))
        out.append(jnp.concatenate([_block_diag(w_a[blk]), _block_diag(w_x[blk])], axis=1))
    return jnp.stack(out).astype(BF16)


def _repack_w_in(w):
    lr0 = 2 * KEY + WIDTH
    tm0 = lr0 + GLA_RANK + 2 * WIDTH + 2 * KEY + WIDTH
    pad = jnp.zeros((D_MODEL, LANES - GLA_RANK), w.dtype)
    w_nat = jnp.concatenate([w[:, :lr0], w[:, lr0 + GLA_RANK:tm0], w[:, lr0:lr0 + GLA_RANK], pad], axis=1)
    return w_nat.astype(BF16), w[:, tm0:].astype(BF16)


def _mixers_call(h3, p, consts, cos_t, sin_t):
    batch, seq, _ = h3.shape
    n_tiles = seq // CHUNK
    time_spec =pl.BlockSpec((CHUNK, LANES), lambda g, j: (j, 0))
    c = consts
    operands = [
        (h3, pl.BlockSpec((GROUP, CHUNK, D_MODEL), lambda g, j: (g, 0, 0))),
        (h3, pl.BlockSpec((GROUP, CHUNK, D_MODEL), lambda g, j: (g, jnp.minimum(j + 1, n_tiles - 1), 0))),
        (p['norm_gain'], None), (p['w_nat'], None), (p['w_tm'], None),
        (p['w_lr'], None), (p['b_lr'], None),
        (p['gla_gain'], None), (p['ret_gain'], None), (p['ret_bias'], None),
        (cos_t, time_spec), (sin_t, time_spec),
        (c['tri'], None), (c['cmask'], None), (c['avg'], None), (c['rmask'], None),
        (c['rkw'], None), (c['rqw'], None), (c['rdecq'], None), (c['rquad'], None),
        (p['conv_w'], None), (p['conv_b'], None), (p['w_lru'], None), (p['b_a'], None), (p['b_x'], None),
        (p['lam'], None),
        (p['s5_wb'], None), (p['s5_wc'], None), (p['s5_lam'], None), (p['s5_d'], None),
        (p['glu_w'], None), (p['glu_b'], None),
    ]
    args = [a for a, _ in operands]
    specs = [s if s is not None else _const_spec(a.shape) for a, s in operands]
    scratch = [
        pltpu.VMEM((ROWS, D_MODEL), BF16),
        pltpu.VMEM((ROWS, D_MODEL), BF16),
        pltpu.VMEM((ROWS, N_NAT), F32),
        pltpu.VMEM((ROWS, N_TM), F32),
        pltpu.VMEM((GROUP * PAIRS, 2 * DV, LANES), F32),
        pltpu.VMEM((GROUP * PAIRS, LANES, 2 * DV), F32),
        pltpu.VMEM((ROWS + CONV_HEAD, WIDTH), F32),
        pltpu.VMEM((ROWS, WIDTH), F32),
        pltpu.VMEM((ROWS, WIDTH), F32),
        pltpu.VMEM((GROUP, WIDTH), F32),
        pltpu.VMEM((2, ROWS, 2 * S5_SLAB_STATE), F32),
        pltpu.VMEM((2, ROWS, 2 * S5_SLAB_STATE), BF16),
        pltpu.VMEM((N_SLABS, 2, GROUP, S5_SLAB_STATE), F32),
        pltpu.VMEM((D_MODEL // LANES, ROWS, LANES), F32),
        pltpu.VMEM((D_MODEL // LANES, ROWS, LANES), F32),
    ]
    return pl.pallas_call(
        _mixers_kernel,
        grid=(batch // GROUP, seq // CHUNK),
        in_specs=specs,
        out_specs=pl.BlockSpec((GROUP, CHUNK, N_BRANCHES * WIDTH), lambda g, j: (g, j, 0)),
        out_shape=jax.ShapeDtypeStruct((batch, seq, N_BRANCHES * WIDTH), BF16),
        scratch_shapes=scratch,
        compiler_params=pltpu.CompilerParams(dimension_semantics=("arbitrary", "arbitrary"),
                                             vmem_limit_bytes=VMEM_LIMIT_BYTES),
        name="mixers",
    )(*args)


def _merge_call(h2, ocat, p, final_gain, final_norm):
    rows = h2.shape[0]
    tile = min(MERGE_TILE, rows)
    row_spec = lambda cols: pl.BlockSpec((tile, cols), lambda i, j: (i, 0))
    args = [h2, ocat, p['norm_gain'], p['w_gate'], p['w_branch'], p['w_out'], final_gain]
    specs = [row_spec(D_MODEL), row_spec(N_BRANCHES * WIDTH)] + [_const_spec(a.shape) for a in args[2:]]
    return pl.pallas_call(
        functools.partial(_merge_kernel, final_norm=final_norm),
        grid=(rows // tile, 1),
        in_specs=specs,
        out_specs=row_spec(D_MODEL),
        out_shape=jax.ShapeDtypeStruct((rows, D_MODEL), F32),
        scratch_shapes=[pltpu.VMEM((tile, D_MODEL), BF16), pltpu.VMEM((tile, D_MODEL), BF16)],
        compiler_params=pltpu.CompilerParams(dimension_semantics=("arbitrary", "arbitrary"),
                                             vmem_limit_bytes=VMEM_LIMIT_BYTES),
        name="merge",
    )(*args)


@jax.jit
def _forward(x, norm_gain, w_in, gla_w_lr, gla_b_lr, gla_norm_gain, ret_norm_gain, ret_norm_bias,
             lru_conv_w, lru_conv_b, lru_w_a, lru_b_a, lru_w_x, lru_b_x, lru_lambda,
             s5_a_re, s5_a_im, s5_b_re, s5_b_im, s5_c_re, s5_c_im, s5_d, s5_log_dt, s5_glu_w, s5_glu_b,
             w_merge_gate, w_branch, w_out, final_norm_gain):
    batch, seq, _ = x.shape
    depth = w_in.shape[0]
    assert batch % GROUP == 0 and seq % CHUNK == 0 and (batch * seq) % min(MERGE_TILE, batch * seq) == 0
    consts = _tile_constants()
    cos_t, sin_t = _rope_tables(seq)
    row = lambda v: v.astype(F32)[None, :]
    h3 = x.astype(F32)
    for l in range(depth):
        s5_wb, s5_wc, s5_lam = _s5_params(s5_a_re[l], s5_a_im[l], s5_b_re[l], s5_b_im[l],
                                          s5_c_re[l], s5_c_im[l], s5_log_dt[l])
        w_nat, w_tm = _repack_w_in(w_in[l])
        p = {
            'norm_gain': row(norm_gain[l]),
            'w_nat': w_nat, 'w_tm': w_tm,
            'w_lr': jnp.concatenate([gla_w_lr[l], jnp.zeros((LANES - GLA_RANK, KEY), F32)], axis=0).astype(BF16),
            'b_lr': row(gla_b_lr[l]),
            'gla_gain': row(gla_norm_gain[l]),
            'ret_gain': row(ret_norm_gain[l]),
            'ret_bias': row(ret_norm_bias[l]),
            'conv_w': lru_conv_w[l].astype(F32),
            'conv_b': row(lru_conv_b[l]),
            'w_lru': _lru_weights(lru_w_a[l], lru_w_x[l]),
            'b_a': row(lru_b_a[l]),
            'b_x': row(lru_b_x[l]),
            'lam': row(lru_lambda[l]),
            's5_wb': s5_wb, 's5_wc': s5_wc, 's5_lam': s5_lam,
            's5_d': row(s5_d[l]),
            'glu_w': s5_glu_w[l].astype(BF16),
            'glu_b': row(s5_glu_b[l]),
            'w_gate': w_merge_gate[l].astype(BF16),
            'w_branch': w_branch[l].astype(BF16),
            'w_out': w_out[l].astype(BF16),
        }
        ocat = _mixers_call(h3, p, consts, cos_t, sin_t)
        h2 = _merge_call(h3.reshape(batch * seq, D_MODEL), ocat.reshape(batch * seq, N_BRANCHES * WIDTH), p,
                         row(final_norm_gain), final_norm=(l == depth - 1))
        h3 = h2.reshape(batch, seq, D_MODEL)
    return h3.astype(x.dtype)


def kernel(x, norm_gain, w_in, gla_w_lr, gla_b_lr, gla_norm_gain, ret_norm_gain, ret_norm_bias, lru_conv_w,
           lru_conv_b, lru_w_a, lru_b_a, lru_w_x, lru_b_x, lru_lambda, s5_a_re, s5_a_im, s5_b_re, s5_b_im,
           s5_c_re, s5_c_im, s5_d, s5_log_dt, s5_glu_w, s5_glu_b, w_merge_gate, w_branch, w_out,
           final_norm_gain):
    return _forward(x, norm_gain, w_in, gla_w_lr, gla_b_lr, gla_norm_gain, ret_norm_gain, ret_norm_bias,
                    lru_conv_w, lru_conv_b, lru_w_a, lru_b_a, lru_w_x, lru_b_x, lru_lambda,
                    s5_a_re, s5_a_im, s5_b_re, s5_b_im, s5_c_re, s5_c_im, s5_d, s5_log_dt, s5_glu_w, s5_glu_b,
                    w_merge_gate, w_branch, w_out, final_norm_gain)
```

```python
import functools
import math

import jax
import jax.numpy as jnp
from jax import lax
from jax.experimental import pallas as pl
from jax.experimental.pallas import tpu as pltpu

F32 = jnp.float32
BF16 = jnp.bfloat16

D_MODEL = 1024
N_BRANCHES = 4
WIDTH = 512
NORM_EPS = 1e-6
CHUNK = 64
HEADS = 4
DK = 64
DV = WIDTH // HEADS
KEY = HEADS * DK
GLA_RANK = 16
GLA_GATE_TEMP = 16.0
ROPE_BASE = 10000.0
LRU_BLOCKS = 8
LRU_BLOCK_DIM = WIDTH // LRU_BLOCKS
CONV_WIDTH = 4
LRU_C = 8.0
S5_GROUP = 16
S5_GROUPS = WIDTH // S5_GROUP
S5_STATE = 64

LANES = 128
SUBLANES = 8
VMEM_LIMIT_BYTES = 60 * 1024 * 1024

GROUP = SUBLANES
ROWS = GROUP * CHUNK
SUB = 256
SUB_CHUNKS = SUB // CHUNK
N_SUB = ROWS // SUB
PAIRS = HEADS // 2

C_GQ, C_GK, C_GV, C_GG = 0, 256, 512, 1024
C_RQ, C_RK, C_RV, C_RG = 1536, 1792, 2048, 2560
C_LR = 3072
N_NAT = C_LR + LANES
T_LX, T_LG, T_SU, T_SG = 0, 512, 1024, 1536
N_TM = 2048
PROJ_BLOCK = 512
NAT_PIECE = 256
N_SLABS = WIDTH // LANES
S5_SLAB_STATE = (LANES // S5_GROUP) * S5_STATE
CONV_HEAD = (CONV_WIDTH - 1) * GROUP

MERGE_TILE = 512
MERGE_COLS = 256


def _sigmoid(x):
    return jax.nn.sigmoid(x)


def _silu(x):
    return x * jax.nn.sigmoid(x)


def _gelu_tanh(x):
    c = math.sqrt(2.0 / math.pi)
    return 0.5 * x * (1.0 + jnp.tanh(c * (x + 0.044715 * (x * x * x))))


def _dot(a, b):
    return jnp.dot(a, b, preferred_element_type=F32)


def _dot_nt(a, b):
    return lax.dot_general(a, b, (((1,), (1,)), ((), ())), preferred_element_type=F32)


def _dot_tn(a, b):
    return lax.dot_general(a, b, (((0,), (0,)), ((), ())), preferred_element_type=F32)


def _rmsnorm_rows(x, gain):
    return x * lax.rsqrt(jnp.mean(x * x, axis=-1, keepdims=True) + NORM_EPS) * gain


def _split_bf16(x):
    hi = x.astype(BF16)
    lo = (x - hi.astype(F32)).astype(BF16)
    return hi, lo


def _lane_mean(x, avg):
    return _dot(x.astype(BF16), avg)


def _rmsnorm_wide(x, gain, avg):
    n_slab = x.shape[1] // LANES
    sq = x * x
    part = sq[:, 0:LANES]
    for i in range(1, n_slab):
        part = part + sq[:, LANES * i:LANES * (i + 1)]
    scale = lax.rsqrt(_lane_mean(part, avg) * (1.0 / n_slab) + NORM_EPS)
    return jnp.concatenate([x[:, LANES * i:LANES * (i + 1)] * scale for i in range(n_slab)], axis=1) * gain


def _mixers_kernel(hfirst_ref, hnext_ref, ng_ref, wnat_ref, wtm_ref, wlr_ref, blr_ref, gng_ref, rng_ref, rnb_ref,
                   cos_ref, sin_ref, tri_ref, cmask_ref, avg_ref, rmask_ref, rkw_ref, rqw_ref, rdecq_ref, rquad_ref,
                   convw_ref, convb_ref, wlru_ref, ba_ref, bx_ref, lam_ref,
                   wb_ref, wc_ref, s5lam_ref, s5d_ref, gluw_ref, glub_ref,
                   ocat_ref,
                   ub_ref, ut_ref, proj_ref, projt_ref, gst_ref, rst_ref, lxbuf_ref, la_ref, lg_ref, lcar_ref,
                   xs_ref, hs_ref, scar_ref, perm_ref, stage_ref):
    j = pl.program_id(1)

    def normalise_tile(src_ref):
        x = src_ref[...].reshape(ROWS, D_MODEL)
        ub_ref[...] = _rmsnorm_wide(x, ng_ref[...], avg_ref[0:LANES, 0:LANES]).astype(BF16)
        for i in range(D_MODEL // LANES):
            for b in range(GROUP):
                stage_ref[i, pl.ds(b, CHUNK, stride=GROUP), :] = (
                    ub_ref[CHUNK * b:CHUNK * (b + 1), LANES * i:LANES * (i + 1)].astype(F32))
            ut_ref[:, LANES * i:LANES * (i + 1)] = stage_ref[i].astype(BF16)

    @pl.when(j == 0)
    def _():
        normalise_tile(hfirst_ref)
        gst_ref[...] = jnp.zeros_like(gst_ref)
        rst_ref[...] = jnp.zeros_like(rst_ref)
        lxbuf_ref[0:CONV_HEAD, :] = jnp.zeros((CONV_HEAD, WIDTH), F32)
        lcar_ref[...] = jnp.zeros_like(lcar_ref)
        scar_ref[...] = jnp.zeros_like(scar_ref)

    @pl.when(j != 0)
    def _():
        lxbuf_ref[0:CONV_HEAD, :] = lxbuf_ref[ROWS:ROWS + CONV_HEAD, :]

    for c0 in range(0, N_TM, PROJ_BLOCK):
        projt_ref[:, c0:c0 + PROJ_BLOCK] = _dot(ut_ref[...], wtm_ref[:, c0:c0 + PROJ_BLOCK])

    nat_blocks = [(c0, c0 + NAT_PIECE) for c0 in range(0, C_LR, NAT_PIECE)] + [(C_LR, N_NAT)]

    def project_next_block():
        if nat_blocks:
            c0, c1 = nat_blocks.pop(0)
            proj_ref[:, c0:c1] = _dot(ub_ref[...], wnat_ref[:, c0:c1])

    lxbuf_ref[CONV_HEAD:CONV_HEAD + ROWS, :] = projt_ref[:, T_LX:T_LX + WIDTH]
    for s in range(N_SLABS):
        ls = slice(LANES * s, LANES * (s + 1))
        xc = convb_ref[:, ls] + convw_ref[0:1, ls] * lxbuf_ref[0:ROWS, ls]
        for tap in range(1, CONV_WIDTH):
            xc = xc + convw_ref[tap:tap + 1, ls] * lxbuf_ref[GROUP * tap:GROUP * tap + ROWS, ls]
        ri = _dot(xc.astype(BF16), wlru_ref[s])
        r = _sigmoid(ri[:, :LANES] + ba_ref[:, ls])
        i_gate = _sigmoid(ri[:, LANES:] + bx_ref[:, ls])
        log_a = -LRU_C * r * jax.nn.softplus(-lam_ref[:, ls])
        a = jnp.exp(log_a)
        la_ref[:, ls] = a
        lg_ref[:, ls] = xc * i_gate * jnp.sqrt(-jnp.tanh(log_a) * (a * a + 1.0))
        project_next_block()

    def lru_body(t, hprev):
        b0 = pl.multiple_of(t * GROUP, GROUP)
        hnew = la_ref[pl.ds(b0, GROUP), :] * hprev + lg_ref[pl.ds(b0, GROUP), :]
        lg_ref[pl.ds(b0, GROUP), :] = hnew
        return hnew

    lcar_ref[...] = lax.fori_loop(0, CHUNK, lru_body, lcar_ref[...], unroll=True)
    for s in range(N_SLABS):
        ls = slice(LANES * s, LANES * (s + 1))
        perm_ref[s] = lg_ref[:, ls] * _silu(projt_ref[:, T_LG + LANES * s:T_LG + LANES * (s + 1)])
    project_next_block()

    half = S5_SLAB_STATE
    pairs_t = CHUNK // 2
    for s in range(N_SLABS):
        ls = slice(LANES * s, LANES * (s + 1))
        buf = s % 2
        us = projt_ref[:, T_SU + LANES * s:T_SU + LANES * (s + 1)]
        us3 = us.reshape(pairs_t, 2 * GROUP, LANES)
        u_eo = jnp.concatenate([us3[:, 0:GROUP, :].reshape(pairs_t * GROUP, LANES),
                                us3[:, GROUP:2 * GROUP, :].reshape(pairs_t * GROUP, LANES)], axis=1).astype(BF16)
        xs_ref[buf] = _dot(u_eo, wb_ref[s])
        project_next_block()
        lam_r = s5lam_ref[s, 0]
        lam_i = s5lam_ref[s, 1]

        def s5_body(t4, carry, buf=buf, lam_r=lam_r, lam_i=lam_i):
            hr, hi = carry
            b0 = pl.multiple_of(t4 * 2 * GROUP, 2 * GROUP)
            before_r, before_i = [], []
            for k in range(2):
                before_r.append(hr)
                before_i.append(hi)
                xr = xs_ref[buf, pl.ds(b0 + GROUP * k, GROUP), 0:half]
                xi = xs_ref[buf, pl.ds(b0 + GROUP * k, GROUP), half:2 * half]
                hr, hi = lam_r * hr - lam_i * hi + xr, lam_r * hi + lam_i * hr + xi
            hs_ref[buf, pl.ds(b0, 2 * GROUP), 0:half] = jnp.concatenate(before_r, axis=0).astype(BF16)
            hs_ref[buf, pl.ds(b0, 2 * GROUP), half:2 * half] = jnp.concatenate(before_i, axis=0).astype(BF16)
            return hr, hi

        cr, ci = lax.fori_loop(0, pairs_t // 2, s5_body, (scar_ref[s, 0], scar_ref[s, 1]), unroll=True)
        scar_ref[s, 0] = cr
        scar_ref[s, 1] = ci
        y_eo = _dot(hs_ref[buf], wc_ref[s, 0:2 * half, :]) + _dot(u_eo, wc_ref[s, 2 * half:, :])
        y = jnp.concatenate([y_eo[:, 0:LANES].reshape(pairs_t, GROUP, LANES),
                             y_eo[:, LANES:2 * LANES].reshape(pairs_t, GROUP, LANES)], axis=1).reshape(ROWS, LANES)
        lg_ref[:, ls] = _gelu_tanh(y + s5d_ref[:, ls] * us)
        project_next_block()
    y5 = lg_ref[...]
    glu = _sigmoid(_dot(y5.astype(BF16), gluw_ref[...]) + glub_ref[...])
    o_s5 = y5 * glu * _silu(projt_ref[:, T_SG:T_SG + WIDTH])
    for s in range(N_SLABS):
        perm_ref[N_SLABS + s] = o_s5[:, LANES * s:LANES * (s + 1)]

    for i in range(2 * N_SLABS):
        for b in range(GROUP):
            ocat_ref[b, :, 2 * WIDTH + LANES * i:2 * WIDTH + LANES * (i + 1)] = (
                perm_ref[i, pl.ds(b, CHUNK, stride=GROUP), :].astype(BF16))
    while nat_blocks:
        project_next_block()

    normalise_tile(hnext_ref)

    lane = lax.broadcasted_iota(jnp.int32, (1, LANES), 1)
    head_masks = (lane < DK, lane >= DK)
    cmask = cmask_ref[...] != 0.0
    avg = avg_ref[...]
    quad_t = ((lax.broadcasted_iota(jnp.int32, (2 * DV, LANES), 0) < DV)
              == (lax.broadcasted_iota(jnp.int32, (2 * DV, LANES), 1) < DK))
    lane_k = lax.broadcasted_iota(jnp.int32, (1, KEY), 1)
    first_half = (lane_k % DK) < (DK // 2)
    cos_t = jnp.concatenate([cos_ref[...]] * SUB_CHUNKS, axis=0)
    cos_t = jnp.concatenate([cos_t] * (KEY // LANES), axis=1)
    sin_t = jnp.concatenate([sin_ref[...]] * SUB_CHUNKS, axis=0)
    sin_t = jnp.concatenate([sin_t] * (KEY // LANES), axis=1)

    def rope(t):
        swapped = jnp.where(first_half, pltpu.roll(t, KEY - DK // 2, 1), pltpu.roll(t, DK // 2, 1))
        return t * cos_t + swapped * sin_t

    finished = []
    for st in range(N_SUB):
        rows = slice(st * SUB, (st + 1) * SUB)
        out_rows = slice(SUB_CHUNKS * st, SUB_CHUNKS * (st + 1))

        lr = proj_ref[rows, C_LR:C_LR + LANES].astype(BF16)
        z = _dot(lr, wlr_ref[...]) + blr_ref[...]
        log_a = jax.nn.log_sigmoid(z) * (1.0 / GLA_GATE_TEMP)
        la_hl = jnp.concatenate(_split_bf16(log_a), axis=1)
        cs = _dot(tri_ref[...], la_hl)
        cum = cs[:, :KEY] + cs[:, KEY:]
        tot = jnp.concatenate(
            [jnp.broadcast_to(cum[CHUNK * (c + 1) - 1:CHUNK * (c + 1), :], (CHUNK, KEY)) for c in range(SUB_CHUNKS)],
            axis=0)
        gq = proj_ref[rows, C_GQ:C_GQ + KEY]
        gk = proj_ref[rows, C_GK:C_GK + KEY]
        q_dec = gq * (DK ** -0.5) * jnp.exp(cum)
        k_dec = gk * jnp.exp(-cum)
        k_tail = gk * jnp.exp(tot - cum)
        e_dec = jnp.exp(tot)
        qr = rope(proj_ref[rows, C_RQ:C_RQ + KEY])
        kr = rope(proj_ref[rows, C_RK:C_RK + KEY]) * (DK ** -0.5)
        q_w = qr * rqw_ref[...]
        k_w = kr * rkw_ref[...]

        units = []
        for p in range(PAIRS):
            pk = slice(LANES * p, LANES * (p + 1))
            units.append(dict(
                gla=True, p=p, q=q_dec[:, pk], q_inter=q_dec[:, pk].astype(BF16), kd=k_dec[:, pk].astype(BF16),
                kt=k_tail[:, pk].astype(BF16), dec=e_dec[:, pk],
                vp=proj_ref[rows, C_GV + 2 * DV * p:C_GV + 2 * DV * (p + 1)].astype(BF16)))
        for p in range(PAIRS):
            pk = slice(LANES * p, LANES * (p + 1))
            units.append(dict(
                gla=False, p=p, q=qr[:, pk], q_inter=q_w[:, pk].astype(BF16), kd=kr[:, pk].astype(BF16),
                kt=k_w[:, pk].astype(BF16),
                vp=proj_ref[rows, C_RV + 2 * DV * p:C_RV + 2 * DV * (p + 1)].astype(BF16)))

        for u in units:
            u['sc'] = []
            for hh in range(2):
                qm = jnp.where(head_masks[hh], u['q'], 0.0).astype(BF16)
                sc = _dot_nt(qm, u['kd'])
                sc = jnp.where(cmask, sc, 0.0) if u['gla'] else sc * rmask_ref[2 * u['p'] + hh]
                u['sc'].append(sc.astype(BF16))
        for u in units:
            u['o'] = jnp.concatenate([_dot(u['sc'][hh], u['vp'][:, DV * hh:DV * (hh + 1)]) for hh in range(2)], axis=1)
        for u in units:
            outs = []
            for c in range(SUB_CHUNKS):
                cr = slice(CHUNK * c, CHUNK * (c + 1))
                sidx = (st * PAIRS + u['p']) * SUB_CHUNKS + c
                if u['gla']:
                    state = gst_ref[sidx]
                    outs.append(u['o'][cr] + _dot_nt(u['q_inter'][cr], state.astype(BF16)))
                    kv_t = _dot_tn(u['vp'][cr], u['kt'][cr])
                    gst_ref[sidx] = jnp.where(quad_t, state * u['dec'][CHUNK * c:CHUNK * c + 1, :] + kv_t, 0.0)
                else:
                    state = rst_ref[sidx]
                    outs.append(u['o'][cr] + _dot(u['q_inter'][cr], state.astype(BF16)))
                    kv = _dot_tn(u['kt'][cr], u['vp'][cr])
                    rst_ref[sidx] = state * rdecq_ref[u['p']] + kv * rquad_ref[...]
            u['oh'] = jnp.concatenate(outs, axis=0)
            u['rows'], u['out_rows'] = rows, out_rows
        finished.append(units)

    for units in finished:
        for u in units:
            rows, out_rows = u['rows'], u['out_rows']
            pc = slice(2 * DV * u['p'], 2 * DV * (u['p'] + 1))
            oh = u['oh']
            if u['gla']:
                oh = oh * lax.rsqrt(_lane_mean(oh * oh, avg) + NORM_EPS) * gng_ref[:, pc]
                gate = proj_ref[rows, C_GG + 2 * DV * u['p']:C_GG + 2 * DV * (u['p'] + 1)]
                out_cols = pc
            else:
                cen = oh - _lane_mean(oh, avg)
                oh = cen * lax.rsqrt(_lane_mean(cen * cen, avg) + NORM_EPS)
                oh = oh * rng_ref[:, pc] + rnb_ref[:, pc]
                gate = proj_ref[rows, C_RG + 2 * DV * u['p']:C_RG + 2 * DV * (u['p'] + 1)]
                out_cols = slice(WIDTH + 2 * DV * u['p'], WIDTH + 2 * DV * (u['p'] + 1))
            ocat_ref[out_rows, :, out_cols] = (oh * _silu(gate)).astype(BF16).reshape(SUB_CHUNKS, CHUNK, 2 * DV)


def _merge_kernel(h_ref, ocat_ref, ng_ref, wg_ref, wbr_ref, wout_ref, fg_ref, out_ref, ub_ref, m_ref,
                  *, final_norm):
    x = h_ref[...]
    ub_ref[...] = _rmsnorm_rows(x, ng_ref[...]).astype(BF16)
    for nb in range(D_MODEL // MERGE_COLS):
        cols = slice(MERGE_COLS * nb, MERGE_COLS * (nb + 1))
        merged = None
        for bi in range(N_BRANCHES):
            gate = _sigmoid(_dot(ub_ref[...], wg_ref[bi, :, cols]))
            branch = _dot(ocat_ref[:, WIDTH * bi:WIDTH * (bi + 1)], wbr_ref[bi, :, cols])
            term = gate * branch
            merged = term if merged is None else merged + term
        m_ref[:, cols] = merged.astype(BF16)
    out = x + _dot(m_ref[...], wout_ref[...])
    if final_norm:
        out = _rmsnorm_rows(out, fg_ref[...])
    out_ref[...] = out


def _const_spec(shape):
    nd = len(shape)
    return pl.BlockSpec(shape, lambda b, j, _nd=nd: (0,) * _nd, pipeline_mode=pl.Buffered(1))


def _block_diag(blocks):
    n, r, c = blocks.shape
    eye = jnp.eye(n, dtype=blocks.dtype)
    return (blocks[:, :, None, :] * eye[:, None, :, None]).reshape(n * r, n * c)


def _tile_constants():
    pos = jnp.arange(SUB)
    same_chunk = (pos[:, None] // CHUNK) == (pos[None, :] // CHUNK)
    causal = same_chunk & (pos[:, None] >= pos[None, :])
    tri = causal.astype(BF16)
    cmask = causal.astype(F32)
    avg = _block_diag(jnp.full((2, LANES, LANES), 1.0 / LANES, F32)).astype(BF16)
    log_gamma = jnp.log1p(-jnp.exp2(-5.0 - jnp.arange(HEADS, dtype=F32)))
    rel = (pos[:, None] - pos[None, :]).astype(F32)
    decay = jnp.exp(jnp.maximum(rel, 0.0)[None] * log_gamma[:, None, None])
    rmask = jnp.where(causal[None], decay, 0.0).astype(F32)
    cpos = (pos % CHUNK).astype(F32)
    k_w = jnp.exp((CHUNK - 1.0 - cpos)[:, None] * log_gamma[None, :])
    q_w = jnp.exp((cpos + 1.0)[:, None] * log_gamma[None, :])
    rkw = jnp.repeat(k_w, DK, axis=1)
    rqw = jnp.repeat(q_w, DK, axis=1)
    rquad = _block_diag(jnp.ones((2, DK, DV), F32))
    rdecq = jnp.exp(CHUNK * log_gamma).reshape(PAIRS, 2)
    rdecq = jnp.stack([_block_diag(rdecq[p][:, None, None] * jnp.ones((2, DK, DV), F32)) for p in range(PAIRS)])
    return dict(tri=tri, cmask=cmask, avg=avg, rmask=rmask, rkw=rkw, rqw=rqw, rdecq=rdecq, rquad=rquad)


def _rope_tables(seq):
    half = DK // 2
    inv = ROPE_BASE ** (-jnp.arange(half, dtype=F32) / half)
    ang = jnp.arange(seq, dtype=F32)[:, None] * inv[None, :]
    cos = jnp.cos(ang)
    sin = jnp.sin(ang)
    cos_t = jnp.concatenate([cos, cos] * (LANES // DK), axis=1)
    sin_t = jnp.concatenate([-sin, sin] * (LANES // DK), axis=1)
    return cos_t, sin_t


def _s5_params(a_re, a_im, b_re, b_im, c_re, c_im, log_dt):
    step = jnp.exp(log_dt)[:, None]
    mag = jnp.exp(step * a_re)
    ab_re = mag * jnp.cos(step * a_im)
    ab_im = mag * jnp.sin(step * a_im)
    den = a_re * a_re + a_im * a_im
    f_re = ((ab_re - 1.0) * a_re + ab_im * a_im) / den
    f_im = (ab_im * a_re - (ab_re - 1.0) * a_im) / den
    bb_re = f_re[..., None] * b_re - f_im[..., None] * b_im
    bb_im = f_re[..., None] * b_im + f_im[..., None] * b_re
    lb_re = ab_re[..., None] * bb_re - ab_im[..., None] * bb_im
    lb_im = ab_re[..., None] * bb_im + ab_im[..., None] * bb_re
    l2_re = ab_re * ab_re - ab_im * ab_im
    l2_im = 2.0 * ab_re * ab_im

    def c_times(k_re, k_im):
        return (c_re * k_re[:, None, :] - c_im * k_im[:, None, :],
                c_re * k_im[:, None, :] + c_im * k_re[:, None, :])

    def re_c_times(m_re, m_im):
        return jnp.einsum('gip,gpj->gij', c_re, m_re) - jnp.einsum('gip,gpj->gij', c_im, m_im)

    m1_re, m1_im = c_times(ab_re, ab_im)
    m2_re, m2_im = c_times(l2_re, l2_im)
    k0 = re_c_times(bb_re, bb_im)
    k1 = re_c_times(lb_re, lb_im)
    gps = LANES // S5_GROUP
    t = lambda a: jnp.swapaxes(a, 1, 2)
    wb, wc = [], []
    for s in range(N_SLABS):
        g = slice(gps * s, gps * (s + 1))
        wb.append(jnp.concatenate([
            jnp.concatenate([_block_diag(t(lb_re[g])), _block_diag(t(lb_im[g]))], axis=1),
            jnp.concatenate([_block_diag(t(bb_re[g])), _block_diag(t(bb_im[g]))], axis=1),
        ], axis=0))
        k0_bd = _block_diag(t(k0[g]))
        wc.append(jnp.concatenate([
            jnp.concatenate([_block_diag(t(m1_re[g])), _block_diag(t(m2_re[g]))], axis=1),
            jnp.concatenate([_block_diag(t(-m1_im[g])), _block_diag(t(-m2_im[g]))], axis=1),
            jnp.concatenate([k0_bd, _block_diag(t(k1[g]))], axis=1),
            jnp.concatenate([jnp.zeros_like(k0_bd), k0_bd], axis=1),
        ], axis=0))
    wb = jnp.stack(wb).astype(BF16)
    wc = jnp.stack(wc).astype(BF16)
    lam2 = jnp.stack([l2_re.reshape(N_SLABS, S5_SLAB_STATE), l2_im.reshape(N_SLABS, S5_SLAB_STATE)], axis=1)
    lam2 = jnp.broadcast_to(lam2[:, :, None, :], (N_SLABS, 2, GROUP, S5_SLAB_STATE)).astype(F32)
    return wb, wc, lam2


def _lru_weights(w_a, w_x):
    per = LANES // LRU_BLOCK_DIM
    out = []
    for s in range(N_SLABS):
        blk = slice(per * s, per * (s + 1))
        out.append(jnp.concatenate([_block_diag(w_a[blk]), _block_diag(w_x[blk])], axis=1))
    return jnp.stack(out).astype(BF16)


def _repack_w_in(w):
    lr0 = 2 * KEY + WIDTH
    tm0 = lr0 + GLA_RANK + 2 * WIDTH + 2 * KEY + WIDTH
    pad = jnp.zeros((D_MODEL, LANES - GLA_RANK), w.dtype)
    w_nat = jnp.concatenate([w[:, :lr0], w[:, lr0 + GLA_RANK:tm0], w[:, lr0:lr0 + GLA_RANK], pad], axis=1)
    return w_nat.astype(BF16), w[:, tm0:].astype(BF16)


def _mixers_call(h3, p, consts, cos_t, sin_t):
    batch, seq, _ = h3.shape
    n_tiles = seq // CHUNK
    time_spec =pl.BlockSpec((CHUNK, LANES), lambda g, j: (j, 0))
    c = consts
    operands = [
        (h3, pl.BlockSpec((GROUP, CHUNK, D_MODEL), lambda g, j: (g, 0, 0))),
        (h3, pl.BlockSpec((GROUP, CHUNK, D_MODEL), lambda g, j: (g, jnp.minimum(j + 1, n_tiles - 1), 0))),
        (p['norm_gain'], None), (p['w_nat'], None), (p['w_tm'], None),
        (p['w_lr'], None), (p['b_lr'], None),
        (p['gla_gain'], None), (p['ret_gain'], None), (p['ret_bias'], None),
        (cos_t, time_spec), (sin_t, time_spec),
        (c['tri'], None), (c['cmask'], None), (c['avg'], None), (c['rmask'], None),
        (c['rkw'], None), (c['rqw'], None), (c['rdecq'], None), (c['rquad'], None),
        (p['conv_w'], None), (p['conv_b'], None), (p['w_lru'], None), (p['b_a'], None), (p['b_x'], None),
        (p['lam'], None),
        (p['s5_wb'], None), (p['s5_wc'], None), (p['s5_lam'], None), (p['s5_d'], None),
        (p['glu_w'], None), (p['glu_b'], None),
    ]
    args = [a for a, _ in operands]
    specs = [s if s is not None else _const_spec(a.shape) for a, s in operands]
    scratch = [
        pltpu.VMEM((ROWS, D_MODEL), BF16),
        pltpu.VMEM((ROWS, D_MODEL), BF16),
        pltpu.VMEM((ROWS, N_NAT), F32),
        pltpu.VMEM((ROWS, N_TM), F32),
        pltpu.VMEM((GROUP * PAIRS, 2 * DV, LANES), F32),
        pltpu.VMEM((GROUP * PAIRS, LANES, 2 * DV), F32),
        pltpu.VMEM((ROWS + CONV_HEAD, WIDTH), F32),
        pltpu.VMEM((ROWS, WIDTH), F32),
        pltpu.VMEM((ROWS, WIDTH), F32),
        pltpu.VMEM((GROUP, WIDTH), F32),
        pltpu.VMEM((2, ROWS // 2, 2 * S5_SLAB_STATE), F32),
        pltpu.VMEM((2, ROWS // 2, 2 * S5_SLAB_STATE), BF16),
        pltpu.VMEM((N_SLABS, 2, GROUP, S5_SLAB_STATE), F32),
        pltpu.VMEM((D_MODEL // LANES, ROWS, LANES), F32),
        pltpu.VMEM((D_MODEL // LANES, ROWS, LANES), F32),
    ]
    return pl.pallas_call(
        _mixers_kernel,
        grid=(batch // GROUP, seq // CHUNK),
        in_specs=specs,
        out_specs=pl.BlockSpec((GROUP, CHUNK, N_BRANCHES * WIDTH), lambda g, j: (g, j, 0)),
        out_shape=jax.ShapeDtypeStruct((batch, seq, N_BRANCHES * WIDTH), BF16),
        scratch_shapes=scratch,
        compiler_params=pltpu.CompilerParams(dimension_semantics=("arbitrary", "arbitrary"),
                                             vmem_limit_bytes=VMEM_LIMIT_BYTES),
        name="mixers",
    )(*args)


def _merge_call(h2, ocat, p, final_gain, final_norm):
    rows = h2.shape[0]
    tile = min(MERGE_TILE, rows)
    row_spec = lambda cols: pl.BlockSpec((tile, cols), lambda i, j: (i, 0))
    args = [h2, ocat, p['norm_gain'], p['w_gate'], p['w_branch'], p['w_out'], final_gain]
    specs = [row_spec(D_MODEL), row_spec(N_BRANCHES * WIDTH)] + [_const_spec(a.shape) for a in args[2:]]
    return pl.pallas_call(
        functools.partial(_merge_kernel, final_norm=final_norm),
        grid=(rows // tile, 1),
        in_specs=specs,
        out_specs=row_spec(D_MODEL),
        out_shape=jax.ShapeDtypeStruct((rows, D_MODEL), F32),
        scratch_shapes=[pltpu.VMEM((tile, D_MODEL), BF16), pltpu.VMEM((tile, D_MODEL), BF16)],
        compiler_params=pltpu.CompilerParams(dimension_semantics=("arbitrary", "arbitrary"),
                                             vmem_limit_bytes=VMEM_LIMIT_BYTES),
        name="merge",
    )(*args)


@jax.jit
def _forward(x, norm_gain, w_in, gla_w_lr, gla_b_lr, gla_norm_gain, ret_norm_gain, ret_norm_bias,
             lru_conv_w, lru_conv_b, lru_w_a, lru_b_a, lru_w_x, lru_b_x, lru_lambda,
             s5_a_re, s5_a_im, s5_b_re, s5_b_im, s5_c_re, s5_c_im, s5_d, s5_log_dt, s5_glu_w, s5_glu_b,
             w_merge_gate, w_branch, w_out, final_norm_gain):
    batch, seq, _ = x.shape
    depth = w_in.shape[0]
    assert batch % GROUP == 0 and seq % CHUNK == 0 and (batch * seq) % min(MERGE_TILE, batch * seq) == 0
    consts = _tile_constants()
    cos_t, sin_t = _rope_tables(seq)
    row = lambda v: v.astype(F32)[None, :]
    h3 = x.astype(F32)
    for l in range(depth):
        s5_wb, s5_wc, s5_lam = _s5_params(s5_a_re[l], s5_a_im[l], s5_b_re[l], s5_b_im[l],
                                          s5_c_re[l], s5_c_im[l], s5_log_dt[l])
        w_nat, w_tm = _repack_w_in(w_in[l])
        p = {
            'norm_gain': row(norm_gain[l]),
            'w_nat': w_nat, 'w_tm': w_tm,
            'w_lr': jnp.concatenate([gla_w_lr[l], jnp.zeros((LANES - GLA_RANK, KEY), F32)], axis=0).astype(BF16),
            'b_lr': row(gla_b_lr[l]),
            'gla_gain': row(gla_norm_gain[l]),
            'ret_gain': row(ret_norm_gain[l]),
            'ret_bias': row(ret_norm_bias[l]),
            'conv_w': lru_conv_w[l].astype(F32),
            'conv_b': row(lru_conv_b[l]),
            'w_lru': _lru_weights(lru_w_a[l], lru_w_x[l]),
            'b_a': row(lru_b_a[l]),
            'b_x': row(lru_b_x[l]),
            'lam': row(lru_lambda[l]),
            's5_wb': s5_wb, 's5_wc': s5_wc, 's5_lam': s5_lam,
            's5_d': row(s5_d[l]),
            'glu_w': s5_glu_w[l].astype(BF16),
            'glu_b': row(s5_glu_b[l]),
            'w_gate': w_merge_gate[l].astype(BF16),
            'w_branch': w_branch[l].astype(BF16),
            'w_out': w_out[l].astype(BF16),
        }
        ocat = _mixers_call(h3, p, consts, cos_t, sin_t)
        h2 = _merge_call(h3.reshape(batch * seq, D_MODEL), ocat.reshape(batch * seq, N_BRANCHES * WIDTH), p,
                         row(final_norm_gain), final_norm=(l == depth - 1))
        h3 = h2.reshape(batch, seq, D_MODEL)
    return h3.astype(x.dtype)


def kernel(x, norm_gain, w_in, gla_w_lr, gla_b_lr, gla_norm_gain, ret_norm_gain, ret_norm_bias, lru_conv_w,
           lru_conv_b, lru_w_a, lru_b_a, lru_w_x, lru_b_x, lru_lambda, s5_a_re, s5_a_im, s5_b_re, s5_b_im,
           s5_c_re, s5_c_im, s5_d, s5_log_dt, s5_glu_w, s5_glu_b, w_merge_gate, w_branch, w_out,
           final_norm_gain):
    return _forward(x, norm_gain, w_in, gla_w_lr, gla_b_lr, gla_norm_gain, ret_norm_gain, ret_norm_bias,
                    lru_conv_w, lru_conv_b, lru_w_a, lru_b_a, lru_w_x, lru_b_x, lru_lambda,
                    s5_a_re, s5_a_im, s5_b_re, s5_b_im, s5_c_re, s5_c_im, s5_d, s5_log_dt, s5_glu_w, s5_glu_b,
                    w_merge_gate, w_branch, w_out, final_norm_gain)
```

```python
import functools
import math

import jax
import jax.numpy as jnp
from jax import lax
from jax.experimental import pallas as pl
from jax.experimental.pallas import tpu as pltpu

F32 = jnp.float32
BF16 = jnp.bfloat16

D_MODEL = 1024
N_BRANCHES = 4
WIDTH = 512
NORM_EPS = 1e-6
CHUNK = 64
HEADS = 4
DK = 64
DV = WIDTH // HEADS
KEY = HEADS * DK
GLA_RANK = 16
GLA_GATE_TEMP = 16.0
ROPE_BASE = 10000.0
LRU_BLOCKS = 8
LRU_BLOCK_DIM = WIDTH // LRU_BLOCKS
CONV_WIDTH = 4
LRU_C = 8.0
S5_GROUP = 16
S5_GROUPS = WIDTH // S5_GROUP
S5_STATE = 64

LANES = 128
SUBLANES = 8
VMEM_LIMIT_BYTES = 60 * 1024 * 1024

GROUP = SUBLANES
ROWS = GROUP * CHUNK
SUB = 256
SUB_CHUNKS = SUB // CHUNK
N_SUB = ROWS // SUB
PAIRS = HEADS // 2

C_GQ, C_GK, C_GV, C_GG = 0, 256, 512, 1024
C_RQ, C_RK, C_RV, C_RG = 1536, 1792, 2048, 2560
C_LR = 3072
N_NAT = C_LR + LANES
T_LX, T_LG, T_SU, T_SG = 0, 512, 1024, 1536
N_TM = 2048
PROJ_BLOCK = 512
NAT_PIECE = 256
N_SLABS = WIDTH // LANES
S5_SLAB_STATE = (LANES // S5_GROUP) * S5_STATE
CONV_HEAD = (CONV_WIDTH - 1) * GROUP

MERGE_TILE = 512
MERGE_COLS = 256


def _sigmoid(x):
    return 0.5 + 0.5 * jnp.tanh(0.5 * x)


def _silu(x):
    hx = 0.5 * x
    return hx + hx * jnp.tanh(hx)


def _gelu_tanh(x):
    c = math.sqrt(2.0 / math.pi)
    return 0.5 * x * (1.0 + jnp.tanh(c * (x + 0.044715 * (x * x * x))))


def _dot(a, b):
    return jnp.dot(a, b, preferred_element_type=F32)


def _dot_nt(a, b):
    return lax.dot_general(a, b, (((1,), (1,)), ((), ())), preferred_element_type=F32)


def _dot_tn(a, b):
    return lax.dot_general(a, b, (((0,), (0,)), ((), ())), preferred_element_type=F32)


def _rmsnorm_rows(x, gain):
    return x * lax.rsqrt(jnp.mean(x * x, axis=-1, keepdims=True) + NORM_EPS) * gain


def _split_bf16(x):
    hi = x.astype(BF16)
    lo = (x - hi.astype(F32)).astype(BF16)
    return hi, lo


def _lane_mean(x, avg):
    return _dot(x.astype(BF16), avg)


def _mixers_kernel(hfirst_ref, hnext_ref, ng_ref, wnat_ref, wtm_ref, wlr_ref, blr_ref, gng_ref, rng_ref, rnb_ref,
                   cos_ref, sin_ref, tri_ref, cmask_ref, avg_ref, rmask_ref, rkw_ref, rqw_ref, rdecq_ref, rquad_ref,
                   convw_ref, convb_ref, wlru_ref, ba_ref, bx_ref, lam_ref,
                   wb_ref, wc_ref, s5lam_ref, s5d_ref, gluw_ref, glub_ref,
                   ocat_ref,
                   ub_ref, ut_ref, proj_ref, projt_ref, gst_ref, rst_ref, lxbuf_ref, la_ref, lg_ref, lcar_ref,
                   xs_ref, hs_ref, scar_ref, perm_ref, stage_ref):
    j = pl.program_id(1)

    def normalise_stages(src_ref):
        n_slab = D_MODEL // LANES
        shared = {}

        def row_scale():
            x = src_ref[...].reshape(ROWS, D_MODEL)
            part = x[:, 0:LANES] * x[:, 0:LANES]
            for i in range(1, n_slab):
                part = part + x[:, LANES * i:LANES * (i + 1)] * x[:, LANES * i:LANES * (i + 1)]
            shared['scale'] = lax.rsqrt(_lane_mean(part, avg_ref[0:LANES, 0:LANES]) * (1.0 / n_slab) + NORM_EPS)

        def slab(i):
            ls = slice(LANES * i, LANES * (i + 1))
            un = src_ref[:, :, ls].reshape(ROWS, LANES) * shared['scale'] * ng_ref[:, ls]
            ub_ref[:, ls] = un.astype(BF16)
            for b in range(GROUP):
                stage_ref[i, pl.ds(b, CHUNK, stride=GROUP), :] = un[CHUNK * b:CHUNK * (b + 1), :]
            ut_ref[:, ls] = stage_ref[i].astype(BF16)

        return [row_scale] + [functools.partial(slab, i) for i in range(n_slab)]

    @pl.when(j == 0)
    def _():
        for stage in normalise_stages(hfirst_ref):
            stage()
        gst_ref[...] = jnp.zeros_like(gst_ref)
        rst_ref[...] = jnp.zeros_like(rst_ref)
        lxbuf_ref[0:CONV_HEAD, :] = jnp.zeros((CONV_HEAD, WIDTH), F32)
        lcar_ref[...] = jnp.zeros_like(lcar_ref)
        scar_ref[...] = jnp.zeros_like(scar_ref)

    @pl.when(j != 0)
    def _():
        lxbuf_ref[0:CONV_HEAD, :] = lxbuf_ref[ROWS:ROWS + CONV_HEAD, :]

    for c0 in range(0, N_TM, PROJ_BLOCK):
        projt_ref[:, c0:c0 + PROJ_BLOCK] = _dot(ut_ref[...], wtm_ref[:, c0:c0 + PROJ_BLOCK])

    nat_blocks = [(c0, c0 + NAT_PIECE) for c0 in range(0, C_LR, NAT_PIECE)] + [(C_LR, N_NAT)]

    def project_next_block():
        if nat_blocks:
            c0, c1 = nat_blocks.pop(0)
            proj_ref[:, c0:c1] = _dot(ub_ref[...], wnat_ref[:, c0:c1])

    lxbuf_ref[CONV_HEAD:CONV_HEAD + ROWS, :] = projt_ref[:, T_LX:T_LX + WIDTH]
    for s in range(N_SLABS):
        ls = slice(LANES * s, LANES * (s + 1))
        xc = convb_ref[:, ls] + convw_ref[0:1, ls] * lxbuf_ref[0:ROWS, ls]
        for tap in range(1, CONV_WIDTH):
            xc = xc + convw_ref[tap:tap + 1, ls] * lxbuf_ref[GROUP * tap:GROUP * tap + ROWS, ls]
        ri = _dot(xc.astype(BF16), wlru_ref[s])
        r = _sigmoid(ri[:, :LANES] + ba_ref[:, ls])
        i_gate = _sigmoid(ri[:, LANES:] + bx_ref[:, ls])
        log_a = -LRU_C * r * jax.nn.softplus(-lam_ref[:, ls])
        a = jnp.exp(log_a)
        la_ref[:, ls] = a
        lg_ref[:, ls] = xc * i_gate * jnp.sqrt(-jnp.tanh(log_a) * (a * a + 1.0))
        project_next_block()

    def lru_body(t, hprev):
        b0 = pl.multiple_of(t * GROUP, GROUP)
        hnew = la_ref[pl.ds(b0, GROUP), :] * hprev + lg_ref[pl.ds(b0, GROUP), :]
        lg_ref[pl.ds(b0, GROUP), :] = hnew
        return hnew

    lcar_ref[...] = lax.fori_loop(0, CHUNK, lru_body, lcar_ref[...], unroll=True)
    for s in range(N_SLABS):
        ls = slice(LANES * s, LANES * (s + 1))
        perm_ref[s] = lg_ref[:, ls] * _silu(projt_ref[:, T_LG + LANES * s:T_LG + LANES * (s + 1)])
    project_next_block()

    half = S5_SLAB_STATE
    pairs_t = CHUNK // 2
    for s in range(N_SLABS):
        ls = slice(LANES * s, LANES * (s + 1))
        buf = s % 2
        us = projt_ref[:, T_SU + LANES * s:T_SU + LANES * (s + 1)]
        us3 = us.reshape(pairs_t, 2 * GROUP, LANES)
        u_eo = jnp.concatenate([us3[:, 0:GROUP, :].reshape(pairs_t * GROUP, LANES),
                                us3[:, GROUP:2 * GROUP, :].reshape(pairs_t * GROUP, LANES)], axis=1).astype(BF16)
        xs_ref[buf] = _dot(u_eo, wb_ref[s])
        project_next_block()
        lam_r = s5lam_ref[s, 0]
        lam_i = s5lam_ref[s, 1]

        def s5_body(t4, carry, buf=buf, lam_r=lam_r, lam_i=lam_i):
            hr, hi = carry
            b0 = pl.multiple_of(t4 * 2 * GROUP, 2 * GROUP)
            before_r, before_i = [], []
            for k in range(2):
                before_r.append(hr)
                before_i.append(hi)
                xr = xs_ref[buf, pl.ds(b0 + GROUP * k, GROUP), 0:half]
                xi = xs_ref[buf, pl.ds(b0 + GROUP * k, GROUP), half:2 * half]
                hr, hi = lam_r * hr - lam_i * hi + xr, lam_r * hi + lam_i * hr + xi
            hs_ref[buf, pl.ds(b0, 2 * GROUP), 0:half] = jnp.concatenate(before_r, axis=0).astype(BF16)
            hs_ref[buf, pl.ds(b0, 2 * GROUP), half:2 * half] = jnp.concatenate(before_i, axis=0).astype(BF16)
            return hr, hi

        cr, ci = lax.fori_loop(0, pairs_t // 2, s5_body, (scar_ref[s, 0], scar_ref[s, 1]), unroll=True)
        scar_ref[s, 0] = cr
        scar_ref[s, 1] = ci
        y_eo = _dot(hs_ref[buf], wc_ref[s, 0:2 * half, :]) + _dot(u_eo, wc_ref[s, 2 * half:, :])
        y = jnp.concatenate([y_eo[:, 0:LANES].reshape(pairs_t, GROUP, LANES),
                             y_eo[:, LANES:2 * LANES].reshape(pairs_t, GROUP, LANES)], axis=1).reshape(ROWS, LANES)
        lg_ref[:, ls] = _gelu_tanh(y + s5d_ref[:, ls] * us)
        project_next_block()
    y5 = lg_ref[...]
    glu = _sigmoid(_dot(y5.astype(BF16), gluw_ref[...]) + glub_ref[...])
    o_s5 = y5 * glu * _silu(projt_ref[:, T_SG:T_SG + WIDTH])
    for s in range(N_SLABS):
        perm_ref[N_SLABS + s] = o_s5[:, LANES * s:LANES * (s + 1)]

    for i in range(2 * N_SLABS):
        for b in range(GROUP):
            ocat_ref[b, :, 2 * WIDTH + LANES * i:2 * WIDTH + LANES * (i + 1)] = (
                perm_ref[i, pl.ds(b, CHUNK, stride=GROUP), :].astype(BF16))
    while nat_blocks:
        project_next_block()

    next_stages = normalise_stages(hnext_ref)

    def normalise_next_stage():
        if next_stages:
            next_stages.pop(0)()

    lane = lax.broadcasted_iota(jnp.int32, (1, LANES), 1)
    head_masks = (lane < DK, lane >= DK)
    cmask = cmask_ref[...] != 0.0
    avg = avg_ref[...]
    quad_t = ((lax.broadcasted_iota(jnp.int32, (2 * DV, LANES), 0) < DV)
              == (lax.broadcasted_iota(jnp.int32, (2 * DV, LANES), 1) < DK))
    lane_k = lax.broadcasted_iota(jnp.int32, (1, KEY), 1)
    first_half = (lane_k % DK) < (DK // 2)
    cos_t = jnp.concatenate([cos_ref[...]] * SUB_CHUNKS, axis=0)
    cos_t = jnp.concatenate([cos_t] * (KEY // LANES), axis=1)
    sin_t = jnp.concatenate([sin_ref[...]] * SUB_CHUNKS, axis=0)
    sin_t = jnp.concatenate([sin_t] * (KEY // LANES), axis=1)

    def rope(t):
        swapped = jnp.where(first_half, pltpu.roll(t, KEY - DK // 2, 1), pltpu.roll(t, DK // 2, 1))
        return t * cos_t + swapped * sin_t

    finished = []
    for st in range(N_SUB):
        rows = slice(st * SUB, (st + 1) * SUB)
        out_rows = slice(SUB_CHUNKS * st, SUB_CHUNKS * (st + 1))

        lr = proj_ref[rows, C_LR:C_LR + LANES].astype(BF16)
        z = _dot(lr, wlr_ref[...]) + blr_ref[...]
        log_a = jax.nn.log_sigmoid(z) * (1.0 / GLA_GATE_TEMP)
        la_hl = jnp.concatenate(_split_bf16(log_a), axis=1)
        cs = _dot(tri_ref[...], la_hl)
        cum = cs[:, :KEY] + cs[:, KEY:]
        tot = jnp.concatenate(
            [jnp.broadcast_to(cum[CHUNK * (c + 1) - 1:CHUNK * (c + 1), :], (CHUNK, KEY)) for c in range(SUB_CHUNKS)],
            axis=0)
        gq = proj_ref[rows, C_GQ:C_GQ + KEY]
        gk = proj_ref[rows, C_GK:C_GK + KEY]
        q_dec = gq * (DK ** -0.5) * jnp.exp(cum)
        k_dec = gk * jnp.exp(-cum)
        k_tail = gk * jnp.exp(tot - cum)
        e_dec = jnp.exp(tot)
        qr = rope(proj_ref[rows, C_RQ:C_RQ + KEY])
        kr = rope(proj_ref[rows, C_RK:C_RK + KEY]) * (DK ** -0.5)
        q_w = qr * rqw_ref[...]
        k_w = kr * rkw_ref[...]

        units = []
        for p in range(PAIRS):
            pk = slice(LANES * p, LANES * (p + 1))
            units.append(dict(
                gla=True, p=p, q=q_dec[:, pk], q_inter=q_dec[:, pk].astype(BF16), kd=k_dec[:, pk].astype(BF16),
                kt=k_tail[:, pk].astype(BF16), dec=e_dec[:, pk],
                vp=proj_ref[rows, C_GV + 2 * DV * p:C_GV + 2 * DV * (p + 1)].astype(BF16)))
        for p in range(PAIRS):
            pk = slice(LANES * p, LANES * (p + 1))
            units.append(dict(
                gla=False, p=p, q=qr[:, pk], q_inter=q_w[:, pk].astype(BF16), kd=kr[:, pk].astype(BF16),
                kt=k_w[:, pk].astype(BF16),
                vp=proj_ref[rows, C_RV + 2 * DV * p:C_RV + 2 * DV * (p + 1)].astype(BF16)))

        for u in units:
            u['sc'] = []
            for hh in range(2):
                qm = jnp.where(head_masks[hh], u['q'], 0.0).astype(BF16)
                sc = _dot_nt(qm, u['kd'])
                sc = jnp.where(cmask, sc, 0.0) if u['gla'] else sc * rmask_ref[2 * u['p'] + hh]
                u['sc'].append(sc.astype(BF16))
            normalise_next_stage()
        for u in units:
            u['o'] = jnp.concatenate([_dot(u['sc'][hh], u['vp'][:, DV * hh:DV * (hh + 1)]) for hh in range(2)], axis=1)
        for u in units:
            outs = []
            for c in range(SUB_CHUNKS):
                cr = slice(CHUNK * c, CHUNK * (c + 1))
                sidx = (st * PAIRS + u['p']) * SUB_CHUNKS + c
                if u['gla']:
                    state = gst_ref[sidx]
                    outs.append(u['o'][cr] + _dot_nt(u['q_inter'][cr], state.astype(BF16)))
                    kv_t = _dot_tn(u['vp'][cr], u['kt'][cr])
                    gst_ref[sidx] = jnp.where(quad_t, state * u['dec'][CHUNK * c:CHUNK * c + 1, :] + kv_t, 0.0)
                else:
                    state = rst_ref[sidx]
                    outs.append(u['o'][cr] + _dot(u['q_inter'][cr], state.astype(BF16)))
                    kv = _dot_tn(u['kt'][cr], u['vp'][cr])
                    rst_ref[sidx] = state * rdecq_ref[u['p']] + kv * rquad_ref[...]
            u['oh'] = jnp.concatenate(outs, axis=0)
            u['rows'], u['out_rows'] = rows, out_rows
        finished.append(units)
    while next_stages:
        normalise_next_stage()

    for units in finished:
        for u in units:
            rows, out_rows = u['rows'], u['out_rows']
            pc = slice(2 * DV * u['p'], 2 * DV * (u['p'] + 1))
            oh = u['oh']
            if u['gla']:
                oh = oh * lax.rsqrt(_lane_mean(oh * oh, avg) + NORM_EPS) * gng_ref[:, pc]
                gate = proj_ref[rows, C_GG + 2 * DV * u['p']:C_GG + 2 * DV * (u['p'] + 1)]
                out_cols = pc
            else:
                cen = oh - _lane_mean(oh, avg)
                oh = cen * lax.rsqrt(_lane_mean(cen * cen, avg) + NORM_EPS)
                oh = oh * rng_ref[:, pc] + rnb_ref[:, pc]
                gate = proj_ref[rows, C_RG + 2 * DV * u['p']:C_RG + 2 * DV * (u['p'] + 1)]
                out_cols = slice(WIDTH + 2 * DV * u['p'], WIDTH + 2 * DV * (u['p'] + 1))
            ocat_ref[out_rows, :, out_cols] = (oh * _silu(gate)).astype(BF16).reshape(SUB_CHUNKS, CHUNK, 2 * DV)


def _merge_kernel(h_ref, ocat_ref, ng_ref, wg_ref, wbr_ref, wout_ref, fg_ref, out_ref, ub_ref, m_ref,
                  *, final_norm):
    x = h_ref[...]
    ub_ref[...] = _rmsnorm_rows(x, ng_ref[...]).astype(BF16)
    for nb in range(D_MODEL // MERGE_COLS):
        cols = slice(MERGE_COLS * nb, MERGE_COLS * (nb + 1))
        merged = None
        for bi in range(N_BRANCHES):
            gate = _sigmoid(_dot(ub_ref[...], wg_ref[bi, :, cols]))
            branch = _dot(ocat_ref[:, WIDTH * bi:WIDTH * (bi + 1)], wbr_ref[bi, :, cols])
            term = gate * branch
            merged = term if merged is None else merged + term
        m_ref[:, cols] = merged.astype(BF16)
    out = x + _dot(m_ref[...], wout_ref[...])
    if final_norm:
        out = _rmsnorm_rows(out, fg_ref[...])
    out_ref[...] = out


def _const_spec(shape):
    nd = len(shape)
    return pl.BlockSpec(shape, lambda b, j, _nd=nd: (0,) * _nd, pipeline_mode=pl.Buffered(1))


def _layer_spec(shape, layer):
    nd = len(shape)
    return pl.BlockSpec((None,) + tuple(shape[1:]), lambda b, j, _nd=nd: (layer,) + (0,) * (_nd - 1),
                        pipeline_mode=pl.Buffered(1))


def _block_diag(blocks):
    *lead, n, r, c = blocks.shape
    eye = jnp.eye(n, dtype=blocks.dtype)
    return (blocks[..., :, :, None, :] * eye[:, None, :, None]).reshape(*lead, n * r, n * c)


def _tile_constants():
    pos = jnp.arange(SUB)
    same_chunk = (pos[:, None] // CHUNK) == (pos[None, :] // CHUNK)
    causal = same_chunk & (pos[:, None] >= pos[None, :])
    tri = causal.astype(BF16)
    cmask = causal.astype(F32)
    avg = _block_diag(jnp.full((2, LANES, LANES), 1.0 / LANES, F32)).astype(BF16)
    log_gamma = jnp.log1p(-jnp.exp2(-5.0 - jnp.arange(HEADS, dtype=F32)))
    rel = (pos[:, None] - pos[None, :]).astype(F32)
    decay = jnp.exp(jnp.maximum(rel, 0.0)[None] * log_gamma[:, None, None])
    rmask = jnp.where(causal[None], decay, 0.0).astype(F32)
    cpos = (pos % CHUNK).astype(F32)
    k_w = jnp.exp((CHUNK - 1.0 - cpos)[:, None] * log_gamma[None, :])
    q_w = jnp.exp((cpos + 1.0)[:, None] * log_gamma[None, :])
    rkw = jnp.repeat(k_w, DK, axis=1)
    rqw = jnp.repeat(q_w, DK, axis=1)
    rquad = _block_diag(jnp.ones((2, DK, DV), F32))
    rdecq = _block_diag(jnp.exp(CHUNK * log_gamma).reshape(PAIRS, 2, 1, 1) * jnp.ones((PAIRS, 2, DK, DV), F32))
    return dict(tri=tri, cmask=cmask, avg=avg, rmask=rmask, rkw=rkw, rqw=rqw, rdecq=rdecq, rquad=rquad)


def _rope_tables(seq):
    half = DK // 2
    inv = ROPE_BASE ** (-jnp.arange(half, dtype=F32) / half)
    ang = jnp.arange(seq, dtype=F32)[:, None] * inv[None, :]
    cos = jnp.cos(ang)
    sin = jnp.sin(ang)
    cos_t = jnp.concatenate([cos, cos] * (LANES // DK), axis=1)
    sin_t = jnp.concatenate([-sin, sin] * (LANES // DK), axis=1)
    return cos_t, sin_t


def _s5_params(a_re, a_im, b_re, b_im, c_re, c_im, log_dt):
    step = jnp.exp(log_dt)[..., None]
    mag = jnp.exp(step * a_re)
    ab_re = mag * jnp.cos(step * a_im)
    ab_im = mag * jnp.sin(step * a_im)
    den = a_re * a_re + a_im * a_im
    f_re = ((ab_re - 1.0) * a_re + ab_im * a_im) / den
    f_im = (ab_im * a_re - (ab_re - 1.0) * a_im) / den
    bb_re = f_re[..., None] * b_re - f_im[..., None] * b_im
    bb_im = f_re[..., None] * b_im + f_im[..., None] * b_re
    lb_re = ab_re[..., None] * bb_re - ab_im[..., None] * bb_im
    lb_im = ab_re[..., None] * bb_im + ab_im[..., None] * bb_re
    l2_re = ab_re * ab_re - ab_im * ab_im
    l2_im = 2.0 * ab_re * ab_im

    def c_times(k_re, k_im):
        return (c_re * k_re[..., None, :] - c_im * k_im[..., None, :],
                c_re * k_im[..., None, :] + c_im * k_re[..., None, :])

    def re_c_times(m_re, m_im):
        return jnp.einsum('lgip,lgpj->lgij', c_re, m_re) - jnp.einsum('lgip,lgpj->lgij', c_im, m_im)

    m1_re, m1_im = c_times(ab_re, ab_im)
    m2_re, m2_im = c_times(l2_re, l2_im)
    k0 = re_c_times(bb_re, bb_im)
    k1 = re_c_times(lb_re, lb_im)
    depth = a_re.shape[0]
    gps = LANES // S5_GROUP

    def slabs(blocks):
        bt = jnp.swapaxes(blocks, -1, -2)
        return _block_diag(bt.reshape(depth, N_SLABS, gps, bt.shape[-2], bt.shape[-1]))

    k0_bd = slabs(k0)
    wb = jnp.concatenate([
        jnp.concatenate([slabs(lb_re), slabs(lb_im)], axis=-1),
        jnp.concatenate([slabs(bb_re), slabs(bb_im)], axis=-1),
    ], axis=-2).astype(BF16)
    wc = jnp.concatenate([
        jnp.concatenate([slabs(m1_re), slabs(m2_re)], axis=-1),
        jnp.concatenate([slabs(-m1_im), slabs(-m2_im)], axis=-1),
        jnp.concatenate([k0_bd, slabs(k1)], axis=-1),
        jnp.concatenate([jnp.zeros_like(k0_bd), k0_bd], axis=-1),
    ], axis=-2).astype(BF16)
    lam2 = jnp.stack([l2_re.reshape(depth, N_SLABS, S5_SLAB_STATE),
                      l2_im.reshape(depth, N_SLABS, S5_SLAB_STATE)], axis=2)
    lam2 = jnp.broadcast_to(lam2[:, :, :, None, :], (depth, N_SLABS, 2, GROUP, S5_SLAB_STATE)).astype(F32)
    return wb, wc, lam2


def _lru_weights(w_a, w_x):
    per = LANES // LRU_BLOCK_DIM
    pair = lambda w: _block_diag(w.reshape(w.shape[0], N_SLABS, per, LRU_BLOCK_DIM, LRU_BLOCK_DIM))
    return jnp.concatenate([pair(w_a), pair(w_x)], axis=-1).astype(BF16)


def _repack_w_in(w):
    lr0 = 2 * KEY + WIDTH
    tm0 = lr0 + GLA_RANK + 2 * WIDTH + 2 * KEY + WIDTH
    pad = jnp.zeros(w.shape[:2] + (LANES - GLA_RANK,), w.dtype)
    w_nat = jnp.concatenate([w[..., :lr0], w[..., lr0 + GLA_RANK:tm0], w[..., lr0:lr0 + GLA_RANK], pad], axis=-1)
    return w_nat.astype(BF16), w[..., tm0:].astype(BF16)


def _mixers_call(h3, p, layer, consts, cos_t, sin_t):
    batch, seq, _ = h3.shape
    n_tiles = seq // CHUNK
    time_spec = pl.BlockSpec((CHUNK, LANES), lambda g, j: (j, 0))
    c = consts
    per_layer = lambda name: (p[name], _layer_spec(p[name].shape, layer))
    const = lambda a: (a, _const_spec(a.shape))
    operands = [
        (h3, pl.BlockSpec((GROUP, CHUNK, D_MODEL), lambda g, j: (g, 0, 0))),
        (h3, pl.BlockSpec((GROUP, CHUNK, D_MODEL), lambda g, j: (g, jnp.minimum(j + 1, n_tiles - 1), 0))),
        per_layer('norm_gain'), per_layer('w_nat'), per_layer('w_tm'),
        per_layer('w_lr'), per_layer('b_lr'),
        per_layer('gla_gain'), per_layer('ret_gain'), per_layer('ret_bias'),
        (cos_t, time_spec), (sin_t, time_spec),
        const(c['tri']), const(c['cmask']), const(c['avg']), const(c['rmask']),
        const(c['rkw']), const(c['rqw']), const(c['rdecq']), const(c['rquad']),
        per_layer('conv_w'), per_layer('conv_b'), per_layer('w_lru'), per_layer('b_a'), per_layer('b_x'),
        per_layer('lam'),
        per_layer('s5_wb'), per_layer('s5_wc'), per_layer('s5_lam'), per_layer('s5_d'),
        per_layer('glu_w'), per_layer('glu_b'),
    ]
    args = [a for a, _ in operands]
    specs = [s for _, s in operands]
    scratch = [
        pltpu.VMEM((ROWS, D_MODEL), BF16),
        pltpu.VMEM((ROWS, D_MODEL), BF16),
        pltpu.VMEM((ROWS, N_NAT), F32),
        pltpu.VMEM((ROWS, N_TM), F32),
        pltpu.VMEM((GROUP * PAIRS, 2 * DV, LANES), F32),
        pltpu.VMEM((GROUP * PAIRS, LANES, 2 * DV), F32),
        pltpu.VMEM((ROWS + CONV_HEAD, WIDTH), F32),
        pltpu.VMEM((ROWS, WIDTH), F32),
        pltpu.VMEM((ROWS, WIDTH), F32),
        pltpu.VMEM((GROUP, WIDTH), F32),
        pltpu.VMEM((2, ROWS // 2, 2 * S5_SLAB_STATE), F32),
        pltpu.VMEM((2, ROWS // 2, 2 * S5_SLAB_STATE), BF16),
        pltpu.VMEM((N_SLABS, 2, GROUP, S5_SLAB_STATE), F32),
        pltpu.VMEM((D_MODEL // LANES, ROWS, LANES), F32),
        pltpu.VMEM((D_MODEL // LANES, ROWS, LANES), F32),
    ]
    return pl.pallas_call(
        _mixers_kernel,
        grid=(batch // GROUP, seq // CHUNK),
        in_specs=specs,
        out_specs=pl.BlockSpec((GROUP, CHUNK, N_BRANCHES * WIDTH), lambda g, j: (g, j, 0)),
        out_shape=jax.ShapeDtypeStruct((batch, seq, N_BRANCHES * WIDTH), BF16),
        scratch_shapes=scratch,
        compiler_params=pltpu.CompilerParams(dimension_semantics=("arbitrary", "arbitrary"),
                                             vmem_limit_bytes=VMEM_LIMIT_BYTES),
        name="mixers",
    )(*args)


def _merge_call(h2, ocat, p, layer, final_gain, final_norm):
    rows = h2.shape[0]
    tile = min(MERGE_TILE, rows)
    row_spec = lambda cols: pl.BlockSpec((tile, cols), lambda i, j: (i, 0))
    layered = [p['norm_gain'], p['w_gate'], p['w_branch'], p['w_out']]
    args = [h2, ocat] + layered + [final_gain]
    specs = ([row_spec(D_MODEL), row_spec(N_BRANCHES * WIDTH)] + [_layer_spec(a.shape, layer) for a in layered]
             + [_const_spec(final_gain.shape)])
    return pl.pallas_call(
        functools.partial(_merge_kernel, final_norm=final_norm),
        grid=(rows // tile, 1),
        in_specs=specs,
        out_specs=row_spec(D_MODEL),
        out_shape=jax.ShapeDtypeStruct((rows, D_MODEL), F32),
        scratch_shapes=[pltpu.VMEM((tile, D_MODEL), BF16), pltpu.VMEM((tile, D_MODEL), BF16)],
        compiler_params=pltpu.CompilerParams(dimension_semantics=("arbitrary", "arbitrary"),
                                             vmem_limit_bytes=VMEM_LIMIT_BYTES),
        name="merge",
    )(*args)


@jax.jit
def _forward(x, norm_gain, w_in, gla_w_lr, gla_b_lr, gla_norm_gain, ret_norm_gain, ret_norm_bias,
             lru_conv_w, lru_conv_b, lru_w_a, lru_b_a, lru_w_x, lru_b_x, lru_lambda,
             s5_a_re, s5_a_im, s5_b_re, s5_b_im, s5_c_re, s5_c_im, s5_d, s5_log_dt, s5_glu_w, s5_glu_b,
             w_merge_gate, w_branch, w_out, final_norm_gain):
    batch, seq, _ = x.shape
    depth = w_in.shape[0]
    assert batch % GROUP == 0 and seq % CHUNK == 0 and (batch * seq) % min(MERGE_TILE, batch * seq) == 0
    consts = _tile_constants()
    cos_t, sin_t = _rope_tables(seq)
    rows = lambda v: v.astype(F32)[:, None, :]
    s5_wb, s5_wc, s5_lam = _s5_params(s5_a_re, s5_a_im, s5_b_re, s5_b_im, s5_c_re, s5_c_im, s5_log_dt)
    w_nat, w_tm = _repack_w_in(w_in)
    p = {
        'norm_gain': rows(norm_gain),
        'w_nat': w_nat, 'w_tm': w_tm,
        'w_lr': jnp.concatenate([gla_w_lr, jnp.zeros((depth, LANES - GLA_RANK, KEY), F32)], axis=1).astype(BF16),
        'b_lr': rows(gla_b_lr),
        'gla_gain': rows(gla_norm_gain),
        'ret_gain': rows(ret_norm_gain),
        'ret_bias': rows(ret_norm_bias),
        'conv_w': lru_conv_w.astype(F32),
        'conv_b': rows(lru_conv_b),
        'w_lru': _lru_weights(lru_w_a, lru_w_x),
        'b_a': rows(lru_b_a),
        'b_x': rows(lru_b_x),
        'lam': rows(lru_lambda),
        's5_wb': s5_wb, 's5_wc': s5_wc, 's5_lam': s5_lam,
        's5_d': rows(s5_d),
        'glu_w': s5_glu_w.astype(BF16),
        'glu_b': rows(s5_glu_b),
        'w_gate': w_merge_gate.astype(BF16),
        'w_branch': w_branch.astype(BF16),
        'w_out': w_out.astype(BF16),
    }
    final_gain = final_norm_gain.astype(F32)[None, :]
    h3 = x.astype(F32)
    for l in range(depth):
        ocat = _mixers_call(h3, p, l, consts, cos_t, sin_t)
        h2 = _merge_call(h3.reshape(batch * seq, D_MODEL), ocat.reshape(batch * seq, N_BRANCHES * WIDTH), p, l,
                         final_gain, final_norm=(l == depth - 1))
        h3 = h2.reshape(batch, seq, D_MODEL)
    return h3.astype(x.dtype)


def kernel(x, norm_gain, w_in, gla_w_lr, gla_b_lr, gla_norm_gain, ret_norm_gain, ret_norm_bias, lru_conv_w,
           lru_conv_b, lru_w_a, lru_b_a, lru_w_x, lru_b_x, lru_lambda, s5_a_re, s5_a_im, s5_b_re, s5_b_im,
           s5_c_re, s5_c_im, s5_d, s5_log_dt, s5_glu_w, s5_glu_b, w_merge_gate, w_branch, w_out,
           final_norm_gain):
    return _forward(x, norm_gain, w_in, gla_w_lr, gla_b_lr, gla_norm_gain, ret_norm_gain, ret_norm_bias,
                    lru_conv_w, lru_conv_b, lru_w_a, lru_b_a, lru_w_x, lru_b_x, lru_lambda,
                    s5_a_re, s5_a_im, s5_b_re, s5_b_im, s5_c_re, s5_c_im, s5_d, s5_log_dt, s5_glu_w, s5_glu_b,
                    w_merge_gate, w_branch, w_out, final_norm_gain)
```
